```python
import jax
import jax.numpy as jnp
from jax import lax
import numpy as np

D_MODEL = 1024
BATCH = 16
SEQ = 2048
DEPTH = 2

CHUNK = 64
MIX_WIDTH = D_MODEL
GROUP_WIDTH = MIX_WIDTH // 4
HEAD_DIM = GROUP_WIDTH // 4
CONV_WIDTH = 4
ROPE_THETA = 10000.0
EPS = 1e-6
RET_HEADS = 4
RET_DK = HEAD_DIM
RET_DV = HEAD_DIM
LRU_WIDTH = GROUP_WIDTH
LRU_BLOCKS = 4
LRU_BLOCK = LRU_WIDTH // LRU_BLOCKS
LRU_C = 8.0
MLA_HEADS = 4
MLA_NOPE = HEAD_DIM
MLA_ROPE = HEAD_DIM // 2
MLA_V = HEAD_DIM
MLA_Q_LORA = 3 * GROUP_WIDTH // 4
MLA_KV_LORA = GROUP_WIDTH // 2
Q_BLOCK = 128
GDN_HEADS = 4
GDN_DK = HEAD_DIM
GDN_DV = HEAD_DIM
GDN_QKV = 2 * GDN_HEADS * GDN_DK + GDN_HEADS * GDN_DV
N_MEM = 256
XA_HEADS = 4
XA_HEAD_DIM = D_MODEL // XA_HEADS
D_FF = 4 * D_MODEL
IN_SIZES = (
    RET_HEADS * RET_DK, RET_HEADS * RET_DK, RET_HEADS * RET_DV, RET_HEADS * RET_DV,
    LRU_WIDTH, LRU_WIDTH,
    MLA_Q_LORA, MLA_KV_LORA, MLA_ROPE,
    GDN_QKV, GDN_HEADS * GDN_DV, GDN_HEADS, GDN_HEADS,
)
IN_COLS = sum(IN_SIZES)

kernel_name = "hymba_style_chunk_causal_hybrid_trunk"

F32 = jnp.float32


def split_cols(z, sizes):
    points = []
    acc = 0
    for s in sizes[:-1]:
        acc += s
        points.append(acc)
    return jnp.split(z, points, axis=-1)


def rmsnorm(x, g):
    xf = x.astype(F32)
    y = xf * lax.rsqrt(jnp.mean(xf * xf, axis=-1, keepdims=True) + EPS)
    return (y * g.astype(F32)).astype(x.dtype)


def l2norm(t):
    return t * lax.rsqrt(jnp.sum(t * t, axis=-1, keepdims=True) + EPS)


def rope(x, pos):
    half = x.shape[-1] // 2
    inv_freq = jnp.power(ROPE_THETA, -jnp.arange(half, dtype=F32) / half)
    ang = pos.astype(F32)[..., None] * inv_freq
    cos = jnp.cos(ang)[:, :, None, :]
    sin = jnp.sin(ang)[:, :, None, :]
    xf = x.astype(F32)
    x1, x2 = xf[..., :half], xf[..., half:]
    return jnp.concatenate([x1 * cos - x2 * sin, x1 * sin + x2 * cos], axis=-1).astype(x.dtype)


def causal_conv(x, w):
    k, c = w.shape
    return lax.conv_general_dilated(
        x, w[:, None, :].astype(x.dtype), window_strides=(1,), padding=[(k - 1, 0)],
        dimension_numbers=("NWC", "WIO", "NWC"), feature_group_count=c)


def retention(q, k, v, gate, pos, gn_gain):
    b, s, _ = q.shape
    n = s // CHUNK
    h = RET_HEADS
    q = rope(q.reshape(b, s, h, RET_DK), pos).astype(F32) * (RET_DK ** -0.5)
    k = rope(k.reshape(b, s, h, RET_DK), pos).astype(F32)
    v = v.reshape(b, s, h, RET_DV).astype(F32)
    log_gamma = jnp.log1p(-jnp.exp2(-5.0 - jnp.arange(h, dtype=F32)))
    idx = jnp.arange(CHUNK, dtype=F32)
    intra_decay = jnp.exp(jnp.abs(idx[:, None] - idx[None, :])[None] * log_gamma[:, None, None])
    xi = jnp.exp((idx[:, None] + 1.0) * log_gamma)
    zeta = jnp.exp((CHUNK - 1.0 - idx)[:, None] * log_gamma)
    chunk_decay = jnp.exp(CHUNK * log_gamma)
    qc = q.reshape(b, n, CHUNK, h, RET_DK)
    kc = k.reshape(b, n, CHUNK, h, RET_DK)
    vc = v.reshape(b, n, CHUNK, h, RET_DV)
    scores = jnp.einsum("bnihd,bnjhd->bnhij", qc, kc) * intra_decay
    intra = jnp.einsum("bnhij,bnjhe->bnihe", scores, vc)

    def step(state, inp):
        q_c, k_c, v_c = inp
        cross = jnp.einsum("bihd,bhde->bihe", q_c, state) * xi[None, :, :, None]
        state = state * chunk_decay[None, :, None, None] + jnp.einsum(
            "bjhd,bjhe->bhde", k_c * zeta[None, :, :, None], v_c)
        return state, cross

    state0 = jnp.zeros((b, h, RET_DK, RET_DV), F32)
    to_chunk_major = lambda t: t.transpose(1, 0, 2, 3, 4)
    _, cross = lax.scan(step, state0, (to_chunk_major(qc), to_chunk_major(kc), to_chunk_major(vc)))
    o = (intra + to_chunk_major(cross)).reshape(b, s, h, RET_DV)
    mu = jnp.mean(o, axis=-1, keepdims=True)
    var = jnp.mean(jnp.square(o - mu), axis=-1, keepdims=True)
    o = (o - mu) * lax.rsqrt(var + EPS) * gn_gain.astype(F32)
    return (o.reshape(b, s, h * RET_DV) * jax.nn.silu(gate.astype(F32))).astype(gate.dtype)


def rg_lru_branch(xb, gb, conv_w, conv_b, wa, ba, wx, bx, lam):
    b, s, w = xb.shape
    xc = (causal_conv(xb, conv_w) + conv_b).astype(F32)
    xr = xc.reshape(b, s, LRU_BLOCKS, LRU_BLOCK)
    r = jax.nn.sigmoid(jnp.einsum("bsnc,ncd->bsnd", xr, wa.astype(F32)).reshape(b, s, w) + ba)
    i = jax.nn.sigmoid(jnp.einsum("bsnc,ncd->bsnd", xr, wx.astype(F32)).reshape(b, s, w) + bx)
    log_a = -LRU_C * r * jax.nn.softplus(-lam.astype(F32))
    a = jnp.exp(log_a)
    u = jnp.sqrt(-jnp.expm1(2.0 * log_a)) * (i * xc)

    def combine(left, right):
        a1, b1 = left
        a2, b2 = right
        return a1 * a2, a2 * b1 + b2

    _, hs = lax.associative_scan(combine, (a, u), axis=1)
    return (hs * jax.nn.gelu(gb.astype(F32))).astype(xb.dtype)


def mla(c_q, c_kv, k_rope_raw, pos, q_norm, w_uq, kv_norm, w_ukv):
    b, s, _ = c_q.shape
    h = MLA_HEADS
    q = (rmsnorm(c_q, q_norm) @ w_uq).reshape(b, s, h, MLA_NOPE + MLA_ROPE)
    q_nope = q[..., :MLA_NOPE]
    q_rope = rope(q[..., MLA_NOPE:], pos)
    kv = (rmsnorm(c_kv, kv_norm) @ w_ukv).reshape(b, s, h, MLA_NOPE + MLA_V)
    k_nope, v = kv[..., :MLA_NOPE], kv[..., MLA_NOPE:]
    k_rope = rope(k_rope_raw[:, :, None, :], pos)[:, :, 0, :]
    scale = (MLA_NOPE + MLA_ROPE) ** -0.5
    nq = s // Q_BLOCK
    qn_blocks = q_nope.reshape(b, nq, Q_BLOCK, h, MLA_NOPE).transpose(1, 0, 2, 3, 4)
    qr_blocks = q_rope.reshape(b, nq, Q_BLOCK, h, MLA_ROPE).transpose(1, 0, 2, 3, 4)
    key_chunk = jnp.arange(s) // CHUNK

    def attend(args):
        qn, qr, blk = args
        sc = (jnp.einsum("bqhd,bkhd->bhqk", qn, k_nope)
              + jnp.einsum("bqhr,bkr->bhqk", qr, k_rope)).astype(F32) * scale
        q_chunk = (blk * Q_BLOCK + jnp.arange(Q_BLOCK)) // CHUNK
        mask = key_chunk[None, :] <= q_chunk[:, None]
        p = jax.nn.softmax(jnp.where(mask, sc, -jnp.inf), axis=-1)
        return jnp.einsum("bhqk,bkhe->bqhe", p.astype(v.dtype), v)

    o = lax.map(attend, (qn_blocks, qr_blocks, jnp.arange(nq)))
    return o.transpose(1, 0, 2, 3, 4).reshape(b, s, h * MLA_V)


def gated_deltanet(qkv, gate, b_raw, a_raw, conv_w, a_log, dt_bias, norm_g):
    b, s, _ = qkv.shape
    h = GDN_HEADS
    n = s // CHUNK
    c = CHUNK
    qkv = jax.nn.silu(causal_conv(qkv, conv_w).astype(F32))
    q, k, v = jnp.split(qkv, [h * GDN_DK, 2 * h * GDN_DK], axis=-1)
    heads = lambda t, d: t.reshape(b, n, c, h, d).transpose(0, 3, 1, 2, 4)
    q = l2norm(heads(q, GDN_DK)) * (GDN_DK ** -0.5)
    k = l2norm(heads(k, GDN_DK))
    v = heads(v, GDN_DV)
    beta = jax.nn.sigmoid(b_raw.astype(F32)).reshape(b, n, c, h).transpose(0, 3, 1, 2)
    g = -jnp.exp(a_log.astype(F32)) * jax.nn.softplus(a_raw.astype(F32) + dt_bias.astype(F32))
    gc = jnp.cumsum(g.reshape(b, n, c, h).transpose(0, 3, 1, 2), axis=-1)
    idx = jnp.arange(c)
    causal = idx[:, None] >= idx[None, :]
    strict = idx[:, None] > idx[None, :]
    decay = jnp.exp(jnp.where(causal, gc[..., :, None] - gc[..., None, :], -jnp.inf))
    kb = k * beta[..., None]
    vb = v * beta[..., None]
    lower = jnp.where(strict, jnp.einsum("bhnid,bhnjd->bhnij", kb, k) * decay, 0.0)
    rhs = jnp.concatenate([vb, kb * jnp.exp(gc)[..., None]], axis=-1)
    sol = lax.linalg.triangular_solve(lower + jnp.eye(c, dtype=F32), rhs,
                                      left_side=True, lower=True, unit_diagonal=True)
    u, w = sol[..., :GDN_DV], sol[..., GDN_DV:]
    attn = jnp.where(causal, jnp.einsum("bhnid,bhnjd->bhnij", q, k) * decay, 0.0)
    g_last = gc[..., -1]
    q_dec = q * jnp.exp(gc)[..., None]
    k_dec = k * jnp.exp(g_last[..., None] - gc)[..., None]

    def step(state, inp):
        u_c, w_c, q_c, k_c, attn_c, gl = inp
        v_new = u_c - jnp.einsum("bhcd,bhde->bhce", w_c, state)
        o_c = jnp.einsum("bhcd,bhde->bhce", q_c, state) + jnp.einsum("bhij,bhje->bhie", attn_c, v_new)
        state = state * jnp.exp(gl)[..., None, None] + jnp.einsum("bhcd,bhce->bhde", k_c, v_new)
        return state, o_c

    cm = lambda t: t.transpose(2, 0, 1, 3, 4)
    state0 = jnp.zeros((b, h, GDN_DK, GDN_DV), F32)
    _, o = lax.scan(step, state0, (cm(u), cm(w), cm(q_dec), cm(k_dec), cm(attn), g_last.transpose(2, 0, 1)))
    o = o.transpose(1, 0, 3, 2, 4).reshape(b, s, h, GDN_DV)
    o = rmsnorm(o, norm_g) * jax.nn.silu(gate.astype(F32).reshape(b, s, h, GDN_DV))
    return o.reshape(b, s, h * GDN_DV).astype(gate.dtype)


def memory_xattn(hq, mem_n, wq, wkv, wo):
    b, s, d = hq.shape
    m = mem_n.shape[1]
    q = (hq @ wq).reshape(b, s, XA_HEADS, XA_HEAD_DIM)
    kv = (mem_n @ wkv).reshape(b, m, 2, XA_HEADS, XA_HEAD_DIM)
    k, v = kv[:, :, 0], kv[:, :, 1]
    sc = jnp.einsum("bqhd,bmhd->bhqm", q, k).astype(F32) * (XA_HEAD_DIM ** -0.5)
    p = jax.nn.softmax(sc, axis=-1)
    o = jnp.einsum("bhqm,bmhd->bqhd", p.astype(v.dtype), v).reshape(b, s, d)
    return o @ wo


def setup_inputs(seed: int = 0) -> dict:
    key = jax.random.key(seed)
    ks = iter(jax.random.split(key, 48))
    L = DEPTH

    def nrm(shape, scale):
        return jax.random.normal(next(ks), shape, F32) * scale

    def gain(shape):
        return 1.0 + nrm(shape, 0.05)

    x = nrm((BATCH, SEQ, D_MODEL), 1.0)
    mem = nrm((BATCH, N_MEM, D_MODEL), 1.0)
    start = jax.random.randint(next(ks), (BATCH, 1), 0, 64, dtype=jnp.int32) * CHUNK
    positions = (start + jnp.arange(SEQ, dtype=jnp.int32)[None, :]).astype(jnp.int32)
    u = jax.random.uniform(next(ks), (L, LRU_WIDTH), F32, minval=0.9, maxval=0.999)
    sg = u ** (1.0 / LRU_C)
    lru_lambda = jnp.log(sg) - jnp.log1p(-sg)
    gdn_a_log = jnp.log(jax.random.uniform(next(ks), (L, GDN_HEADS), F32, minval=1.0, maxval=16.0))
    dt = jnp.exp(jax.random.uniform(next(ks), (L, GDN_HEADS), F32,
                                    minval=float(np.log(1e-3)), maxval=float(np.log(1e-1))))
    gdn_dt_bias = dt + jnp.log(-jnp.expm1(-dt))
    return {
        "x": x,
        "mem": mem,
        "positions": positions,
        "norm_mix": gain((L, D_MODEL)),
        "w_in": nrm((L, D_MODEL, IN_COLS), D_MODEL ** -0.5),
        "ret_gn": gain((L, RET_HEADS, RET_DV)),
        "lru_conv_w": nrm((L, CONV_WIDTH, LRU_WIDTH), CONV_WIDTH ** -0.5),
        "lru_conv_b": nrm((L, LRU_WIDTH), 0.02),
        "lru_wa": nrm((L, LRU_BLOCKS, LRU_BLOCK, LRU_BLOCK), LRU_BLOCK ** -0.5),
        "lru_ba": nrm((L, LRU_WIDTH), 0.02),
        "lru_wx": nrm((L, LRU_BLOCKS, LRU_BLOCK, LRU_BLOCK), LRU_BLOCK ** -0.5),
        "lru_bx": nrm((L, LRU_WIDTH), 0.02),
        "lru_lambda": lru_lambda,
        "mla_q_norm": gain((L, MLA_Q_LORA)),
        "mla_w_uq": nrm((L, MLA_Q_LORA, MLA_HEADS * (MLA_NOPE + MLA_ROPE)), MLA_Q_LORA ** -0.5),
        "mla_kv_norm": gain((L, MLA_KV_LORA)),
        "mla_w_ukv": nrm((L, MLA_KV_LORA, MLA_HEADS * (MLA_NOPE + MLA_V)), MLA_KV_LORA ** -0.5),
        "gdn_conv_w": nrm((L, CONV_WIDTH, GDN_QKV), CONV_WIDTH ** -0.5),
        "gdn_a_log": gdn_a_log,
        "gdn_dt_bias": gdn_dt_bias,
        "gdn_norm": gain((L, GDN_DV)),
        "w_out": nrm((L, MIX_WIDTH, D_MODEL), MIX_WIDTH ** -0.5),
        "norm_xattn": gain((L, D_MODEL)),
        "norm_mem": gain((L, D_MODEL)),
        "xa_wq": nrm((L, D_MODEL, D_MODEL), D_MODEL ** -0.5),
        "xa_wkv": nrm((L, D_MODEL, 2 * D_MODEL), D_MODEL ** -0.5),
        "xa_wo": nrm((L, D_MODEL, D_MODEL), D_MODEL ** -0.5),
        "norm_mlp": gain((L, D_MODEL)),
        "mlp_w1": nrm((L, D_MODEL, D_FF), D_MODEL ** -0.5),
        "mlp_w2": nrm((L, D_FF, D_MODEL), D_FF ** -0.5),
        "final_norm": gain((D_MODEL,)),
    }


def reference(x, mem, positions, norm_mix, w_in, ret_gn, lru_conv_w, lru_conv_b, lru_wa, lru_ba,
              lru_wx, lru_bx, lru_lambda, mla_q_norm, mla_w_uq, mla_kv_norm, mla_w_ukv,
              gdn_conv_w, gdn_a_log, gdn_dt_bias, gdn_norm, w_out, norm_xattn, norm_mem,
              xa_wq, xa_wkv, xa_wo, norm_mlp, mlp_w1, mlp_w2, final_norm):
    for l in range(DEPTH):
        h = rmsnorm(x, norm_mix[l])
        z = h @ w_in[l]
        (rq, rk, rv, rg, lx, lg, cq, ckv, kr, gqkv, gg, gb, ga) = split_cols(z, IN_SIZES)
        y_a = retention(rq, rk, rv, rg, positions, ret_gn[l])
        y_b = rg_lru_branch(lx, lg, lru_conv_w[l], lru_conv_b[l], lru_wa[l], lru_ba[l],
                            lru_wx[l], lru_bx[l], lru_lambda[l])
        y_c = mla(cq, ckv, kr, positions, mla_q_norm[l], mla_w_uq[l], mla_kv_norm[l], mla_w_ukv[l])
        y_d = gated_deltanet(gqkv, gg, gb, ga, gdn_conv_w[l], gdn_a_log[l], gdn_dt_bias[l], gdn_norm[l])
        x = x + jnp.concatenate([y_a, y_b, y_c, y_d], axis=-1) @ w_out[l]
        x = x + memory_xattn(rmsnorm(x, norm_xattn[l]), rmsnorm(mem, norm_mem[l]),
                             xa_wq[l], xa_wkv[l], xa_wo[l])
        h = rmsnorm(x, norm_mlp[l])
        x = x + jnp.square(jax.nn.relu(h @ mlp_w1[l])) @ mlp_w2[l]
    return rmsnorm(x, final_norm)
```

```python
import functools

import jax
import jax.numpy as jnp
from jax import lax
from jax.experimental import pallas as pl
from jax.experimental.pallas import tpu as pltpu

F32 = jnp.float32
BF16 = jnp.bfloat16

D_MODEL = 1024
CHUNK = 64
HEADS = 4
HEAD_DIM = 64
GROUP_WIDTH = HEADS * HEAD_DIM
EPS = 1e-6
ROPE_THETA = 10000.0
LRU_C = 8.0
MLA_Q_LORA = 192
MLA_KV_LORA = 128
MLA_ROPE = 32
MLA_NOPE = 64
XA_HEADS = 4
XA_HEAD_DIM = D_MODEL // XA_HEADS
D_FF = 4 * D_MODEL
FF_SLAB = 1024
ATT_TILE = 256
ROW_TILE = 512
SUBLANES = 8
LANES = 128
VMEM_LIMIT = 56 * 1024 * 1024

ZA_W = 4 * GROUP_WIDTH
ZB_W = 2 * GROUP_WIDTH
ZC_W = 512
ZD_W = 1152
ZC_KV = 256
ZC_KR = 384
Z_W = ZA_W + ZB_W + ZC_W + ZD_W


def _dot(a, b):
    return jnp.dot(a.astype(BF16), b.astype(BF16), preferred_element_type=F32)


def _dot_nt(a, b):
    return lax.dot_general(a.astype(BF16), b.astype(BF16), (((1,), (1,)), ((), ())),
                           preferred_element_type=F32)


def _dot_tn(a, b):
    return lax.dot_general(a.astype(BF16), b.astype(BF16), (((0,), (0,)), ((), ())),
                           preferred_element_type=F32)


def _split3(x):
    x1 = x.astype(BF16)
    r1 = x - x1.astype(F32)
    x2 = r1.astype(BF16)
    x3 = (r1 - x2.astype(F32)).astype(BF16)
    return x1, x2, x3


def _dot3(a, b):
    a1 = a.astype(BF16)
    a2 = (a - a1.astype(F32)).astype(BF16)
    b1 = b.astype(BF16)
    b2 = (b - b1.astype(F32)).astype(BF16)
    d = functools.partial(jnp.dot, preferred_element_type=F32)
    return d(a1, b1) + (d(a1, b2) + d(a2, b1))


def _silu(x):
    return x * jax.nn.sigmoid(x)


def _softplus(x):
    return jnp.maximum(x, 0.0) + jnp.log1p(jnp.exp(-jnp.abs(x)))


def _expm1(y):
    e = jnp.exp(y)
    near = (y > -0.5) & (e != 1.0)
    safe_log = jnp.where(near, jnp.log(e), 1.0)
    return jnp.where(near, (e - 1.0) * y / safe_log, jnp.where(e == 1.0, y, e - 1.0))


def _rms(x, n):
    return x * lax.rsqrt(jnp.sum(x * x, axis=-1, keepdims=True) * (1.0 / n) + EPS)


def _chunk_rows(n):
    return pl.ds(pl.multiple_of(n * CHUNK, CHUNK), CHUNK)


def _causal_conv_chunk(z_ref, n, lo, hi, w):
    cur = z_ref[_chunk_rows(n), lo:hi]
    prev_start = pl.multiple_of(jnp.maximum(n * CHUNK - SUBLANES, 0), SUBLANES)
    prev = z_ref[pl.ds(prev_start, SUBLANES), lo:hi]
    prev = jnp.where(n > 0, prev, 0.0)
    tile = jnp.concatenate([prev, cur], axis=0)
    acc = cur * w[3:4, :]
    for k in (1, 2, 3):
        acc = acc + pltpu.roll(tile, k, 0)[SUBLANES:, :] * w[3 - k:4 - k, :]
    return acc


def _lane_block_swap(x, half, first):
    return jnp.where(first, pltpu.roll(x, LANES - half, 1), pltpu.roll(x, half, 1))


def _in_proj_kernel(x_ref, g_ref, w_ref, za_ref, zb_ref, zc_ref, zd_ref):
    x = x_ref[...]
    h = (_rms(x, D_MODEL) * g_ref[...]).astype(BF16)
    lo = 0
    for ref in (za_ref, zb_ref, zc_ref, zd_ref):
        hi = lo + ref.shape[-1]
        ref[...] = jnp.dot(h, w_ref[:, lo:hi], preferred_element_type=F32)
        lo = hi


def _in_proj(x2d, gain, w_packed):
    t = x2d.shape[0]
    row = lambda w: pl.BlockSpec((ROW_TILE, w), lambda i: (i, 0))
    full = lambda a: pl.BlockSpec(a.shape, lambda i: (0,) * a.ndim)
    return pl.pallas_call(
        _in_proj_kernel,
        grid=(t // ROW_TILE,),
        in_specs=[row(D_MODEL), full(gain), full(w_packed)],
        out_specs=[row(ZA_W), row(ZB_W), row(ZC_W), row(ZD_W)],
        out_shape=[jax.ShapeDtypeStruct((t, w), F32) for w in (ZA_W, ZB_W, ZC_W, ZD_W)],
        compiler_params=pltpu.CompilerParams(dimension_semantics=("parallel",),
                                             vmem_limit_bytes=VMEM_LIMIT),
        name="in_proj",
    )(x2d, gain, w_packed)


def _retention_kernel(z_ref, pos_ref, invf_ref, dmat_ref, xi_ref, zeta_ref, cd_ref, gn_ref,
                      o_ref, state_ref):
    s = z_ref.shape[0]
    lane = lax.broadcasted_iota(jnp.int32, (CHUNK, LANES), 1)
    first = (lane % HEAD_DIM) < (HEAD_DIM // 2)
    lane_wide = lax.broadcasted_iota(jnp.int32, (CHUNK, GROUP_WIDTH), 1)
    first_wide = (lane_wide % HEAD_DIM) < (HEAD_DIM // 2)
    state_ref[...] = jnp.zeros_like(state_ref)

    def rope(x, cos, sin_signed):
        parts = []
        for c in range(GROUP_WIDTH // LANES):
            sl = slice(c * LANES, (c + 1) * LANES)
            xs = _lane_block_swap(x[:, sl], HEAD_DIM // 2, first)
            parts.append(x[:, sl] * cos[:, sl] + xs * sin_signed[:, sl])
        return jnp.concatenate(parts, axis=1)

    def body(n, carry):
        rows = _chunk_rows(n)
        ang = pos_ref[rows, :] * invf_ref[...]
        cos = jnp.cos(ang)
        sin = jnp.sin(ang)
        sin_signed = jnp.where(first_wide, -sin, sin)
        q = rope(z_ref[rows, 0:GROUP_WIDTH], cos, sin_signed) * (HEAD_DIM ** -0.5)
        k = rope(z_ref[rows, GROUP_WIDTH:2 * GROUP_WIDTH], cos, sin_signed)
        v = z_ref[rows, 2 * GROUP_WIDTH:3 * GROUP_WIDTH]
        gate = z_ref[rows, 3 * GROUP_WIDTH:4 * GROUP_WIDTH]
        kz = k * zeta_ref[...]
        outs = []
        for h in range(HEADS):
            sl = slice(h * HEAD_DIM, (h + 1) * HEAD_DIM)
            st = state_ref[h]
            scores = _dot_nt(q[:, sl], k[:, sl]) * dmat_ref[h]
            o = _dot(scores, v[:, sl]) + _dot(q[:, sl], st) * xi_ref[:, sl]
            state_ref[h] = st * cd_ref[:, sl] + _dot_tn(kz[:, sl], v[:, sl])
            mu = jnp.mean(o, axis=-1, keepdims=True)
            var = jnp.mean(jnp.square(o - mu), axis=-1, keepdims=True)
            outs.append((o - mu) * lax.rsqrt(var + EPS) * gn_ref[:, sl])
        o_ref[rows, :] = jnp.concatenate(outs, axis=1) * _silu(gate)
        return carry

    lax.fori_loop(0, s // CHUNK, body, 0)


def _retention(za, pos_f, ret_gn):
    b, s, _ = za.shape
    h = HEADS
    half = HEAD_DIM // 2
    inv_freq = jnp.power(ROPE_THETA, -jnp.arange(half, dtype=F32) / half)
    invf = jnp.tile(inv_freq, 2 * h)[None, :]
    log_gamma = jnp.log1p(-jnp.exp2(-5.0 - jnp.arange(h, dtype=F32)))
    idx = jnp.arange(CHUNK, dtype=F32)
    dmat = jnp.exp(jnp.abs(idx[:, None] - idx[None, :])[None] * log_gamma[:, None, None])
    xi = jnp.repeat(jnp.exp((idx[:, None] + 1.0) * log_gamma), HEAD_DIM, axis=1)
    zeta = jnp.repeat(jnp.exp((CHUNK - 1.0 - idx)[:, None] * log_gamma), HEAD_DIM, axis=1)
    cd = jnp.repeat(jnp.exp(CHUNK * log_gamma), HEAD_DIM)[None, :]
    gn = ret_gn.reshape(1, GROUP_WIDTH)
    full = lambda a: pl.BlockSpec(a.shape, lambda i: (0,) * a.ndim)
    return pl.pallas_call(
        _retention_kernel,
        grid=(b,),
        in_specs=[pl.BlockSpec((None, s, ZA_W), lambda i: (i, 0, 0)),
                  pl.BlockSpec((None, s, 1), lambda i: (i, 0, 0)),
                  full(invf), full(dmat), full(xi), full(zeta), full(cd), full(gn)],
        out_specs=pl.BlockSpec((None, s, GROUP_WIDTH), lambda i: (i, 0, 0)),
        out_shape=jax.ShapeDtypeStruct((b, s, GROUP_WIDTH), F32),
        scratch_shapes=[pltpu.VMEM((h, HEAD_DIM, HEAD_DIM), F32)],
        compiler_params=pltpu.CompilerParams(dimension_semantics=("parallel",),
                                             vmem_limit_bytes=VMEM_LIMIT),
        name="retention",
    )(za, pos_f, invf, dmat, xi, zeta, cd, gn)


def _lru_kernel(z_ref, cw_ref, cb_ref, wa_ref, ba_ref, wx_ref, bx_ref, lam_ref, o_ref, h_ref):
    s = z_ref.shape[0]
    w = GROUP_WIDTH
    row = lax.broadcasted_iota(jnp.int32, (CHUNK, w), 0)
    h_ref[...] = jnp.zeros_like(h_ref)

    def body(n, carry):
        rows = _chunk_rows(n)
        xc = _causal_conv_chunk(z_ref, n, 0, w, cw_ref[...]) + cb_ref[...]
        r = jax.nn.sigmoid(_dot(xc, wa_ref[...]) + ba_ref[...])
        i = jax.nn.sigmoid(_dot(xc, wx_ref[...]) + bx_ref[...])
        log_a = -LRU_C * r * _softplus(-lam_ref[...])
        a = jnp.exp(log_a)
        u = jnp.sqrt(-_expm1(2.0 * log_a)) * (i * xc)
        d = 1
        while d < CHUNK:
            a_sh = jnp.where(row >= d, pltpu.roll(a, d, 0), 1.0)
            u_sh = jnp.where(row >= d, pltpu.roll(u, d, 0), 0.0)
            u = a * u_sh + u
            a = a * a_sh
            d *= 2
        hs = u + a * h_ref[...]
        h_ref[...] = hs[CHUNK - 1:CHUNK, :]
        gate = z_ref[rows, w:2 * w]
        o_ref[rows, :] = hs * jax.nn.gelu(gate, approximate=True)
        return carry

    lax.fori_loop(0, s // CHUNK, body, 0)


def _rg_lru(zb, conv_w, conv_b, wa, ba, wx, bx, lam):
    b, s, _ = zb.shape
    bd = lambda m: jax.scipy.linalg.block_diag(*[m[i] for i in range(m.shape[0])]).astype(BF16)
    vec = lambda v: v.reshape(1, GROUP_WIDTH)
    args = (conv_w, vec(conv_b), bd(wa), vec(ba), bd(wx), vec(bx), vec(lam))
    full = lambda a: pl.BlockSpec(a.shape, lambda i: (0,) * a.ndim)
    return pl.pallas_call(
        _lru_kernel,
        grid=(b,),
        in_specs=[pl.BlockSpec((None, s, ZB_W), lambda i: (i, 0, 0))] + [full(a) for a in args],
        out_specs=pl.BlockSpec((None, s, GROUP_WIDTH), lambda i: (i, 0, 0)),
        out_shape=jax.ShapeDtypeStruct((b, s, GROUP_WIDTH), F32),
        scratch_shapes=[pltpu.VMEM((1, GROUP_WIDTH), F32)],
        compiler_params=pltpu.CompilerParams(dimension_semantics=("parallel",),
                                             vmem_limit_bytes=VMEM_LIMIT),
        name="rg_lru",
    )(zb, *args)


def _mla_kernel(z_ref, pos_ref, invf_ref, qn_ref, wq_ref, kvn_ref, wk_ref, wv_ref, o_ref,
                q_s, k_s, v_s):
    s = z_ref.shape[0]
    nt = s // ATT_TILE
    scale = (MLA_NOPE + MLA_ROPE) ** -0.5
    lane = lax.broadcasted_iota(jnp.int32, (ATT_TILE, LANES), 1)
    first = (lane >= MLA_NOPE) & (lane < MLA_NOPE + MLA_ROPE // 2)

    def rope(x, cos, sin_signed):
        return x * cos + _lane_block_swap(x, MLA_ROPE // 2, first) * sin_signed

    def project(t, carry):
        rows = pl.ds(pl.multiple_of(t * ATT_TILE, ATT_TILE), ATT_TILE)
        ang = pos_ref[rows, :] * invf_ref[...]
        cos = jnp.cos(ang)
        sin = jnp.sin(ang)
        sin_signed = jnp.where(first, -sin, sin)
        cq = (_rms(z_ref[rows, 0:ZC_KV], MLA_Q_LORA) * qn_ref[...]).astype(BF16)
        ckv = (_rms(z_ref[rows, ZC_KV:ZC_KR], MLA_KV_LORA) * kvn_ref[...]).astype(BF16)
        k_rope = rope(z_ref[rows, ZC_KR:ZC_W], cos, sin_signed)
        v_s[rows, :] = jnp.dot(ckv, wv_ref[...], preferred_element_type=F32).astype(BF16)
        for h in range(HEADS):
            q = jnp.dot(cq, wq_ref[h], preferred_element_type=F32)
            q_s[h, rows, :] = rope(q, cos, sin_signed).astype(BF16)
            k = jnp.dot(ckv, wk_ref[h], preferred_element_type=F32)
            k_s[h, rows, :] = (k + k_rope).astype(BF16)
        return carry

    lax.fori_loop(0, nt, project, 0)

    r_id = lax.broadcasted_iota(jnp.int32, (ATT_TILE, ATT_TILE), 0) // CHUNK
    c_id = lax.broadcasted_iota(jnp.int32, (ATT_TILE, ATT_TILE), 1) // CHUNK

    def q_tile(i, carry):
        rows = pl.ds(pl.multiple_of(i * ATT_TILE, ATT_TILE), ATT_TILE)
        outs = []
        for h in range(HEADS):
            q = q_s[h, rows, :]

            def kv_tile(j, st):
                m, l, acc = st
                cols = pl.ds(pl.multiple_of(j * ATT_TILE, ATT_TILE), ATT_TILE)
                sc = lax.dot_general(q, k_s[h, cols, :], (((1,), (1,)), ((), ())),
                                     preferred_element_type=F32) * scale
                visible = (c_id + j * (ATT_TILE // CHUNK)) <= (r_id + i * (ATT_TILE // CHUNK))
                sc = jnp.where(visible, sc, -jnp.inf)
                m_new = jnp.maximum(m, jnp.max(sc, axis=-1, keepdims=True))
                p = jnp.exp(sc - m_new)
                alpha = jnp.exp(m - m_new)
                l = alpha * l + jnp.sum(p, axis=-1, keepdims=True)
                v = v_s[cols, h * HEAD_DIM:(h + 1) * HEAD_DIM]
                acc = alpha * acc + jnp.dot(p.astype(BF16), v, preferred_element_type=F32)
                return m_new, l, acc

            init = (jnp.full((ATT_TILE, 1), -jnp.inf, F32), jnp.zeros((ATT_TILE, 1), F32),
                    jnp.zeros((ATT_TILE, HEAD_DIM), F32))
            _, l, acc = lax.fori_loop(0, i + 1, kv_tile, init)
            outs.append(acc / l)
        o_ref[rows, :] = jnp.concatenate(outs, axis=1)
        return carry

    lax.fori_loop(0, nt, q_tile, 0)


def _mla(zc, pos_f, q_norm, w_uq, kv_norm, w_ukv):
    b, s, _ = zc.shape
    h = HEADS
    half = MLA_ROPE // 2
    inv_freq = jnp.power(ROPE_THETA, -jnp.arange(half, dtype=F32) / half)
    invf = jnp.zeros((1, LANES), F32).at[0, MLA_NOPE:MLA_NOPE + MLA_ROPE].set(jnp.tile(inv_freq, 2))
    qn = jnp.zeros((1, ZC_KV), F32).at[0, :MLA_Q_LORA].set(q_norm)
    kvn = kv_norm.reshape(1, MLA_KV_LORA)
    wq = w_uq.reshape(MLA_Q_LORA, h, MLA_NOPE + MLA_ROPE).transpose(1, 0, 2)
    wq = jnp.pad(wq, ((0, 0), (0, ZC_KV - MLA_Q_LORA), (0, LANES - MLA_NOPE - MLA_ROPE))).astype(BF16)
    wkv = w_ukv.reshape(MLA_KV_LORA, h, MLA_NOPE + HEAD_DIM)
    wk = jnp.pad(wkv[:, :, :MLA_NOPE].transpose(1, 0, 2),
                 ((0, 0), (0, 0), (0, LANES - MLA_NOPE))).astype(BF16)
    wv = wkv[:, :, MLA_NOPE:].reshape(MLA_KV_LORA, h * HEAD_DIM).astype(BF16)
    args = (invf, qn, wq, kvn, wk, wv)
    full = lambda a: pl.BlockSpec(a.shape, lambda i: (0,) * a.ndim)
    return pl.pallas_call(
        _mla_kernel,
        grid=(b,),
        in_specs=[pl.BlockSpec((None, s, ZC_W), lambda i: (i, 0, 0)),
                  pl.BlockSpec((None, s, 1), lambda i: (i, 0, 0))] + [full(a) for a in args],
        out_specs=pl.BlockSpec((None, s, GROUP_WIDTH), lambda i: (i, 0, 0)),
        out_shape=jax.ShapeDtypeStruct((b, s, GROUP_WIDTH), F32),
        scratch_shapes=[pltpu.VMEM((h, s, LANES), BF16), pltpu.VMEM((h, s, LANES), BF16),
                        pltpu.VMEM((s, GROUP_WIDTH), BF16)],
        compiler_params=pltpu.CompilerParams(dimension_semantics=("parallel",),
                                             vmem_limit_bytes=VMEM_LIMIT),
        name="mla",
    )(zc, pos_f, *args)


def _expand_heads(x4, width):
    rows = x4.shape[0]
    lane = lax.broadcasted_iota(jnp.int32, (rows, HEADS * width), 1)
    out = jnp.broadcast_to(x4[:, HEADS - 1:HEADS], (rows, HEADS * width))
    for h in range(HEADS - 2, -1, -1):
        out = jnp.where(lane < (h + 1) * width, jnp.broadcast_to(x4[:, h:h + 1], out.shape), out)
    return out


def _gdn_kernel(z_ref, cw_ref, alog_ref, dtb_ref, ng_ref, o_ref, state_ref):
    s = z_ref.shape[0]
    gw = GROUP_WIDTH
    ri = lax.broadcasted_iota(jnp.int32, (CHUNK, CHUNK), 0)
    ci = lax.broadcasted_iota(jnp.int32, (CHUNK, CHUNK), 1)
    causal = ri >= ci
    strict = ri > ci
    tri = causal.astype(BF16)
    tri_t = (ri <= ci).astype(BF16)
    state_ref[...] = jnp.zeros_like(state_ref)

    def body(n, carry):
        rows = _chunk_rows(n)
        qkv = _silu(_causal_conv_chunk(z_ref, n, 0, 3 * gw, cw_ref[...]))
        gate = z_ref[rows, 3 * gw:4 * gw]
        small = z_ref[rows, 4 * gw:4 * gw + LANES]
        beta_e = jax.nn.sigmoid(_expand_heads(small[:, 0:HEADS], HEAD_DIM))
        a_e = _expand_heads(small[:, HEADS:2 * HEADS], HEAD_DIM)
        g_e = -jnp.exp(alog_ref[...]) * _softplus(a_e + dtb_ref[...])
        g1, g2, g3 = _split3(g_e)
        d = functools.partial(jnp.dot, preferred_element_type=F32)
        gcol = d(tri, g1) + (d(tri, g2) + d(tri, g3))
        outs = []
        for h in range(HEADS):
            sl = slice(h * HEAD_DIM, (h + 1) * HEAD_DIM)
            q = _rms(qkv[:, sl], 1.0) * (HEAD_DIM ** -0.5)
            k = _rms(qkv[:, gw + h * HEAD_DIM:gw + (h + 1) * HEAD_DIM], 1.0)
            v = qkv[:, 2 * gw + h * HEAD_DIM:2 * gw + (h + 1) * HEAD_DIM]
            beta = beta_e[:, sl]
            gc = gcol[:, sl]
            dtn = lambda a, b: lax.dot_general(a, b, (((0,), (0,)), ((), ())),
                                               preferred_element_type=F32)
            grow = dtn(g1[:, sl], tri_t) + (dtn(g2[:, sl], tri_t) + dtn(g3[:, sl], tri_t))
            decay = jnp.exp(jnp.where(causal, gc - grow, -jnp.inf))
            kb = k * beta
            vb = v * beta
            egc = jnp.exp(gc)
            lower = jnp.where(strict, _dot_nt(kb, k) * decay, 0.0)
            sol = jnp.concatenate([vb, kb * egc], axis=1)
            sol = sol - _dot3(lower, sol)
            p = lower
            m = 2
            while m < CHUNK:
                p = _dot3(p, p)
                sol = sol + _dot3(p, sol)
                m *= 2
            u = sol[:, :HEAD_DIM]
            w = sol[:, HEAD_DIM:]
            attn = jnp.where(causal, _dot_nt(q, k) * decay, 0.0)
            gl = gc[CHUNK - 1:CHUNK, :]
            st = state_ref[h]
            v_new = u - _dot(w, st)
            o = _dot(q * egc, st) + _dot(attn, v_new)
            state_ref[h] = st * jnp.exp(gl) + _dot_tn(k * jnp.exp(gl - gc), v_new)
            outs.append(_rms(o, HEAD_DIM) * ng_ref[...])
        o_ref[rows, :] = jnp.concatenate(outs, axis=1) * _silu(gate)
        return carry

    lax.fori_loop(0, s // CHUNK, body, 0)


def _gated_deltanet(zd, conv_w, a_log, dt_bias, norm_g):
    b, s, _ = zd.shape
    alog = jnp.repeat(a_log, HEAD_DIM)[None, :]
    dtb = jnp.repeat(dt_bias, HEAD_DIM)[None, :]
    ng = norm_g.reshape(1, HEAD_DIM)
    args = (conv_w, alog, dtb, ng)
    full = lambda a: pl.BlockSpec(a.shape, lambda i: (0,) * a.ndim)
    return pl.pallas_call(
        _gdn_kernel,
        grid=(b,),
        in_specs=[pl.BlockSpec((None, s, ZD_W), lambda i: (i, 0, 0))] + [full(a) for a in args],
        out_specs=pl.BlockSpec((None, s, GROUP_WIDTH), lambda i: (i, 0, 0)),
        out_shape=jax.ShapeDtypeStruct((b, s, GROUP_WIDTH), F32),
        scratch_shapes=[pltpu.VMEM((HEADS, HEAD_DIM, HEAD_DIM), F32)],
        compiler_params=pltpu.CompilerParams(dimension_semantics=("parallel",),
                                             vmem_limit_bytes=VMEM_LIMIT),
        name="gated_deltanet",
    )(zd, *args)


def _mix_out_kernel(x_ref, ya_ref, yb_ref, yc_ref, yd_ref, wo_ref, g_ref, wq_ref, x1_ref, q_ref):
    acc = x_ref[...]
    for gi, y_ref in enumerate((ya_ref, yb_ref, yc_ref, yd_ref)):
        acc = acc + jnp.dot(y_ref[...].astype(BF16), wo_ref[gi * GROUP_WIDTH:(gi + 1) * GROUP_WIDTH, :],
                            preferred_element_type=F32)
    x1_ref[...] = acc
    hq = (_rms(acc, D_MODEL) * g_ref[...]).astype(BF16)
    q_ref[...] = jnp.dot(hq, wq_ref[...], preferred_element_type=F32).astype(BF16)


def _mix_out(x2d, ys, w_out, gain, wq):
    t = x2d.shape[0]
    row = lambda w: pl.BlockSpec((ROW_TILE, w), lambda i: (i, 0))
    full = lambda a: pl.BlockSpec(a.shape, lambda i: (0,) * a.ndim)
    return pl.pallas_call(
        _mix_out_kernel,
        grid=(t // ROW_TILE,),
        in_specs=[row(D_MODEL)] + [row(GROUP_WIDTH)] * 4 + [full(w_out), full(gain), full(wq)],
        out_specs=[row(D_MODEL), row(D_MODEL)],
        out_shape=[jax.ShapeDtypeStruct((t, D_MODEL), F32), jax.ShapeDtypeStruct((t, D_MODEL), BF16)],
        compiler_params=pltpu.CompilerParams(dimension_semantics=("parallel",),
                                             vmem_limit_bytes=VMEM_LIMIT),
        name="mix_out",
    )(x2d, *ys, w_out, gain, wq)


def _mem_kv_kernel(mem_ref, g_ref, wkv_ref, k_ref, v_ref):
    m = (_rms(mem_ref[...], D_MODEL) * g_ref[...]).astype(BF16)
    k_ref[...] = jnp.dot(m, wkv_ref[:, :D_MODEL], preferred_element_type=F32).astype(BF16)
    v_ref[...] = jnp.dot(m, wkv_ref[:, D_MODEL:], preferred_element_type=F32).astype(BF16)


def _mem_kv(mem, gain, wkv):
    b, m, _ = mem.shape
    full = lambda a: pl.BlockSpec(a.shape, lambda i: (0,) * a.ndim)
    blk = pl.BlockSpec((None, m, D_MODEL), lambda i: (i, 0, 0))
    return pl.pallas_call(
        _mem_kv_kernel,
        grid=(b,),
        in_specs=[blk, full(gain), full(wkv)],
        out_specs=[blk, blk],
        out_shape=[jax.ShapeDtypeStruct((b, m, D_MODEL), BF16)] * 2,
        compiler_params=pltpu.CompilerParams(dimension_semantics=("parallel",),
                                             vmem_limit_bytes=VMEM_LIMIT),
        name="mem_kv",
    )(mem, gain, wkv)


def _xattn_mlp_kernel(x_ref, q_ref, k_ref, v_ref, wo_ref, g_ref, w1_ref, w2_ref, fg_ref, o_ref,
                      *, final_norm):
    x = x_ref[...]
    q = q_ref[...]
    heads = []
    for h in range(XA_HEADS):
        sl = slice(h * XA_HEAD_DIM, (h + 1) * XA_HEAD_DIM)
        sc = lax.dot_general(q[:, sl], k_ref[:, sl], (((1,), (1,)), ((), ())),
                             preferred_element_type=F32) * (XA_HEAD_DIM ** -0.5)
        e = jnp.exp(sc - jnp.max(sc, axis=-1, keepdims=True))
        p = e / jnp.sum(e, axis=-1, keepdims=True)
        heads.append(jnp.dot(p.astype(BF16), v_ref[:, sl], preferred_element_type=F32).astype(BF16))
    o = jnp.concatenate(heads, axis=1)
    x = x + jnp.dot(o, wo_ref[...], preferred_element_type=F32)
    hm = (_rms(x, D_MODEL) * g_ref[...]).astype(BF16)
    for c in range(D_FF // FF_SLAB):
        sl = slice(c * FF_SLAB, (c + 1) * FF_SLAB)
        a = jnp.maximum(jnp.dot(hm, w1_ref[:, sl], preferred_element_type=F32), 0.0)
        x = x + jnp.dot((a * a).astype(BF16), w2_ref[sl, :], preferred_element_type=F32)
    if final_norm:
        x = _rms(x, D_MODEL) * fg_ref[...]
    o_ref[...] = x


def _xattn_mlp(x1, q, k, v, wo, gain, w1, w2, final_gain, final_norm):
    b, s, _ = x1.shape
    m = k.shape[1]
    row = pl.BlockSpec((None, ROW_TILE, D_MODEL), lambda i, j: (i, j, 0))
    mem = pl.BlockSpec((None, m, D_MODEL), lambda i, j: (i, 0, 0))
    const = lambda a: pl.BlockSpec(a.shape, lambda i, j: (0,) * a.ndim, pipeline_mode=pl.Buffered(1))
    return pl.pallas_call(
        functools.partial(_xattn_mlp_kernel, final_norm=final_norm),
        grid=(b, s // ROW_TILE),
        in_specs=[row, row, mem, mem, const(wo), const(gain), const(w1), const(w2), const(final_gain)],
        out_specs=row,
        out_shape=jax.ShapeDtypeStruct((b, s, D_MODEL), F32),
        compiler_params=pltpu.CompilerParams(dimension_semantics=("parallel", "parallel"),
                                             vmem_limit_bytes=VMEM_LIMIT),
        name="xattn_mlp",
    )(x1, q, k, v, wo, gain, w1, w2, final_gain)


def _pack_w_in(w):
    d = w.shape[0]
    zeros = lambda n: jnp.zeros((d, n), w.dtype)
    c0 = ZA_W + ZB_W
    cq = w[:, c0:c0 + MLA_Q_LORA]
    ckv = w[:, c0 + MLA_Q_LORA:c0 + MLA_Q_LORA + MLA_KV_LORA]
    kr = w[:, c0 + MLA_Q_LORA + MLA_KV_LORA:c0 + MLA_Q_LORA + MLA_KV_LORA + MLA_ROPE]
    d0 = c0 + MLA_Q_LORA + MLA_KV_LORA + MLA_ROPE
    rest = w[:, d0:]
    packed = jnp.concatenate([
        w[:, :c0],
        cq, zeros(ZC_KV - MLA_Q_LORA), ckv, zeros(MLA_NOPE), kr, zeros(LANES - MLA_NOPE - MLA_ROPE),
        rest, zeros(ZD_W - rest.shape[1]),
    ], axis=1)
    assert packed.shape[1] == Z_W
    return packed.astype(BF16)


def kernel(x, mem, positions, norm_mix, w_in, ret_gn, lru_conv_w, lru_conv_b, lru_wa, lru_ba, lru_wx, lru_bx, lru_lambda, mla_q_norm, mla_w_uq, mla_kv_norm, mla_w_ukv, gdn_conv_w, gdn_a_log, gdn_dt_bias, gdn_norm, w_out, norm_xattn, norm_mem, xa_wq, xa_wkv, xa_wo, norm_mlp, mlp_w1, mlp_w2, final_norm):
    b, s, d = x.shape
    depth = w_in.shape[0]
    pos_f = positions.astype(F32)[..., None]
    vec = lambda v: v.reshape(1, -1)
    for l in range(depth):
        za, zb, zc, zd = _in_proj(x.reshape(b * s, d), vec(norm_mix[l]), _pack_w_in(w_in[l]))
        seq = lambda z: z.reshape(b, s, z.shape[-1])
        y_a = _retention(seq(za), pos_f, ret_gn[l])
        y_b = _rg_lru(seq(zb), lru_conv_w[l], lru_conv_b[l], lru_wa[l], lru_ba[l], lru_wx[l],
                      lru_bx[l], lru_lambda[l])
        y_c = _mla(seq(zc), pos_f, mla_q_norm[l], mla_w_uq[l], mla_kv_norm[l], mla_w_ukv[l])
        y_d = _gated_deltanet(seq(zd), gdn_conv_w[l], gdn_a_log[l], gdn_dt_bias[l], gdn_norm[l])
        ys = [y.reshape(b * s, GROUP_WIDTH) for y in (y_a, y_b, y_c, y_d)]
        x1, q = _mix_out(x.reshape(b * s, d), ys, w_out[l].astype(BF16), vec(norm_xattn[l]),
                         xa_wq[l].astype(BF16))
        k, v = _mem_kv(mem, vec(norm_mem[l]), xa_wkv[l].astype(BF16))
        x = _xattn_mlp(x1.reshape(b, s, d), q.reshape(b, s, d), k, v, xa_wo[l].astype(BF16),
                       vec(norm_mlp[l]), mlp_w1[l].astype(BF16), mlp_w2[l].astype(BF16),
                       vec(final_norm), l == depth - 1)
    return x
```

```python
import functools

import jax
import jax.numpy as jnp
from jax import lax
from jax.experimental import pallas as pl
from jax.experimental.pallas import tpu as pltpu

F32 = jnp.float32
BF16 = jnp.bfloat16

D_MODEL = 1024
CHUNK = 64
HEADS = 4
HEAD_DIM = 64
GROUP_WIDTH = HEADS * HEAD_DIM
EPS = 1e-6
ROPE_THETA = 10000.0
LRU_C = 8.0
MLA_Q_LORA = 192
MLA_KV_LORA = 128
MLA_ROPE = 32
MLA_NOPE = 64
XA_HEADS = 4
XA_HEAD_DIM = D_MODEL // XA_HEADS
D_FF = 4 * D_MODEL
FF_SLAB = 1024
ATT_TILE = 256
GDN_ROWS = 256
assert GDN_ROWS == GROUP_WIDTH and HEAD_DIM == CHUNK
ROW_TILE = 512
SUBLANES = 8
LANES = 128
VMEM_LIMIT = 56 * 1024 * 1024

ZA_W = 4 * GROUP_WIDTH
ZB_W = 2 * GROUP_WIDTH
ZC_W = 512
ZD_W = 1152
ZC_KV = 256
ZC_KR = 384
Z_W = ZA_W + ZB_W + ZC_W + ZD_W


def _dot(a, b):
    return jnp.dot(a.astype(BF16), b.astype(BF16), preferred_element_type=F32)


def _dot_nt(a, b):
    return lax.dot_general(a.astype(BF16), b.astype(BF16), (((1,), (1,)), ((), ())),
                           preferred_element_type=F32)


def _dot_tn(a, b):
    return lax.dot_general(a.astype(BF16), b.astype(BF16), (((0,), (0,)), ((), ())),
                           preferred_element_type=F32)


def _split3(x):
    x1 = x.astype(BF16)
    r1 = x - x1.astype(F32)
    x2 = r1.astype(BF16)
    x3 = (r1 - x2.astype(F32)).astype(BF16)
    return x1, x2, x3


def _dot3(a, b):
    a1 = a.astype(BF16)
    a2 = (a - a1.astype(F32)).astype(BF16)
    b1 = b.astype(BF16)
    b2 = (b - b1.astype(F32)).astype(BF16)
    d = functools.partial(jnp.dot, preferred_element_type=F32)
    return d(a1, b1) + (d(a1, b2) + d(a2, b1))


def _silu(x):
    return x * jax.nn.sigmoid(x)


def _softplus(x):
    return jnp.maximum(x, 0.0) + jnp.log1p(jnp.exp(-jnp.abs(x)))


def _expm1(y):
    e = jnp.exp(y)
    near = (y > -0.5) & (e != 1.0)
    safe_log = jnp.where(near, jnp.log(e), 1.0)
    return jnp.where(near, (e - 1.0) * y / safe_log, jnp.where(e == 1.0, y, e - 1.0))


def _rms(x, n):
    return x * lax.rsqrt(jnp.sum(x * x, axis=-1, keepdims=True) * (1.0 / n) + EPS)


def _chunk_rows(n):
    return pl.ds(pl.multiple_of(n * CHUNK, CHUNK), CHUNK)


def _causal_conv_rows(z_ref, n, nrows, lo, hi, w):
    cur = z_ref[pl.ds(pl.multiple_of(n * nrows, nrows), nrows), lo:hi]
    prev_start = pl.multiple_of(jnp.maximum(n * nrows - SUBLANES, 0), SUBLANES)
    prev = z_ref[pl.ds(prev_start, SUBLANES), lo:hi]
    prev = jnp.where(n > 0, prev, 0.0)
    tile = jnp.concatenate([prev, cur], axis=0)
    acc = cur * w[3:4, :]
    for k in (1, 2, 3):
        acc = acc + pltpu.roll(tile, k, 0)[SUBLANES:, :] * w[3 - k:4 - k, :]
    return acc


def _lane_block_swap(x, half, first):
    return jnp.where(first, pltpu.roll(x, LANES - half, 1), pltpu.roll(x, half, 1))


def _in_proj_kernel(x_ref, g_ref, w_ref, za_ref, zb_ref, zc_ref, zd_ref):
    x = x_ref[...]
    h = (_rms(x, D_MODEL) * g_ref[...]).astype(BF16)
    lo = 0
    for ref in (za_ref, zb_ref, zc_ref, zd_ref):
        hi = lo + ref.shape[-1]
        ref[...] = jnp.dot(h, w_ref[:, lo:hi], preferred_element_type=F32)
        lo = hi


def _in_proj(x2d, gain, w_packed):
    t = x2d.shape[0]
    row = lambda w: pl.BlockSpec((ROW_TILE, w), lambda i: (i, 0))
    full = lambda a: pl.BlockSpec(a.shape, lambda i: (0,) * a.ndim)
    return pl.pallas_call(
        _in_proj_kernel,
        grid=(t // ROW_TILE,),
        in_specs=[row(D_MODEL), full(gain), full(w_packed)],
        out_specs=[row(ZA_W), row(ZB_W), row(ZC_W), row(ZD_W)],
        out_shape=[jax.ShapeDtypeStruct((t, w), F32) for w in (ZA_W, ZB_W, ZC_W, ZD_W)],
        compiler_params=pltpu.CompilerParams(dimension_semantics=("parallel",),
                                             vmem_limit_bytes=VMEM_LIMIT),
        name="in_proj",
    )(x2d, gain, w_packed)


def _retention_kernel(z_ref, pos_ref, invf_ref, dmat_ref, xi_ref, zeta_ref, cd_ref, gn_ref,
                      o_ref, state_ref):
    s = z_ref.shape[0]
    lane = lax.broadcasted_iota(jnp.int32, (CHUNK, LANES), 1)
    first = (lane % HEAD_DIM) < (HEAD_DIM // 2)
    lane_wide = lax.broadcasted_iota(jnp.int32, (CHUNK, GROUP_WIDTH), 1)
    first_wide = (lane_wide % HEAD_DIM) < (HEAD_DIM // 2)
    state_ref[...] = jnp.zeros_like(state_ref)

    def rope(x, cos, sin_signed):
        parts = []
        for c in range(GROUP_WIDTH // LANES):
            sl = slice(c * LANES, (c + 1) * LANES)
            xs = _lane_block_swap(x[:, sl], HEAD_DIM // 2, first)
            parts.append(x[:, sl] * cos[:, sl] + xs * sin_signed[:, sl])
        return jnp.concatenate(parts, axis=1)

    def body(n, carry):
        rows = _chunk_rows(n)
        ang = pos_ref[rows, :] * invf_ref[...]
        cos = jnp.cos(ang)
        sin = jnp.sin(ang)
        sin_signed = jnp.where(first_wide, -sin, sin)
        q = rope(z_ref[rows, 0:GROUP_WIDTH], cos, sin_signed) * (HEAD_DIM ** -0.5)
        k = rope(z_ref[rows, GROUP_WIDTH:2 * GROUP_WIDTH], cos, sin_signed)
        v = z_ref[rows, 2 * GROUP_WIDTH:3 * GROUP_WIDTH]
        gate = z_ref[rows, 3 * GROUP_WIDTH:4 * GROUP_WIDTH]
        kz = k * zeta_ref[...]
        outs = []
        for h in range(HEADS):
            sl = slice(h * HEAD_DIM, (h + 1) * HEAD_DIM)
            st = state_ref[h]
            scores = _dot_nt(q[:, sl], k[:, sl]) * dmat_ref[h]
            o = _dot(scores, v[:, sl]) + _dot(q[:, sl], st) * xi_ref[:, sl]
            state_ref[h] = st * cd_ref[:, sl] + _dot_tn(kz[:, sl], v[:, sl])
            mu = jnp.mean(o, axis=-1, keepdims=True)
            var = jnp.mean(jnp.square(o - mu), axis=-1, keepdims=True)
            outs.append((o - mu) * lax.rsqrt(var + EPS) * gn_ref[:, sl])
        o_ref[rows, :] = jnp.concatenate(outs, axis=1) * _silu(gate)
        return carry

    lax.fori_loop(0, s // CHUNK, body, 0)


def _retention(za, pos_f, ret_gn):
    b, s, _ = za.shape
    h = HEADS
    half = HEAD_DIM // 2
    inv_freq = jnp.power(ROPE_THETA, -jnp.arange(half, dtype=F32) / half)
    invf = jnp.tile(inv_freq, 2 * h)[None, :]
    log_gamma = jnp.log1p(-jnp.exp2(-5.0 - jnp.arange(h, dtype=F32)))
    idx = jnp.arange(CHUNK, dtype=F32)
    dmat = jnp.exp(jnp.abs(idx[:, None] - idx[None, :])[None] * log_gamma[:, None, None])
    xi = jnp.repeat(jnp.exp((idx[:, None] + 1.0) * log_gamma), HEAD_DIM, axis=1)
    zeta = jnp.repeat(jnp.exp((CHUNK - 1.0 - idx)[:, None] * log_gamma), HEAD_DIM, axis=1)
    cd = jnp.repeat(jnp.exp(CHUNK * log_gamma), HEAD_DIM)[None, :]
    gn = ret_gn.reshape(1, GROUP_WIDTH)
    full = lambda a: pl.BlockSpec(a.shape, lambda i: (0,) * a.ndim)
    return pl.pallas_call(
        _retention_kernel,
        grid=(b,),
        in_specs=[pl.BlockSpec((None, s, ZA_W), lambda i: (i, 0, 0)),
                  pl.BlockSpec((None, s, 1), lambda i: (i, 0, 0)),
                  full(invf), full(dmat), full(xi), full(zeta), full(cd), full(gn)],
        out_specs=pl.BlockSpec((None, s, GROUP_WIDTH), lambda i: (i, 0, 0)),
        out_shape=jax.ShapeDtypeStruct((b, s, GROUP_WIDTH), F32),
        scratch_shapes=[pltpu.VMEM((h, HEAD_DIM, HEAD_DIM), F32)],
        compiler_params=pltpu.CompilerParams(dimension_semantics=("parallel",),
                                             vmem_limit_bytes=VMEM_LIMIT),
        name="retention",
    )(za, pos_f, invf, dmat, xi, zeta, cd, gn)


def _lru_kernel(z_ref, cw_ref, cb_ref, wa_ref, ba_ref, wx_ref, bx_ref, lam_ref, o_ref, h_ref):
    s = z_ref.shape[0]
    w = GROUP_WIDTH
    row = lax.broadcasted_iota(jnp.int32, (CHUNK, w), 0)
    h_ref[...] = jnp.zeros_like(h_ref)

    def body(n, carry):
        rows = _chunk_rows(n)
        xc = _causal_conv_rows(z_ref, n, CHUNK, 0, w, cw_ref[...]) + cb_ref[...]
        r = jax.nn.sigmoid(_dot(xc, wa_ref[...]) + ba_ref[...])
        i = jax.nn.sigmoid(_dot(xc, wx_ref[...]) + bx_ref[...])
        log_a = -LRU_C * r * _softplus(-lam_ref[...])
        a = jnp.exp(log_a)
        u = jnp.sqrt(-_expm1(2.0 * log_a)) * (i * xc)
        d = 1
        while d < CHUNK:
            a_sh = jnp.where(row >= d, pltpu.roll(a, d, 0), 1.0)
            u_sh = jnp.where(row >= d, pltpu.roll(u, d, 0), 0.0)
            u = a * u_sh + u
            a = a * a_sh
            d *= 2
        hs = u + a * h_ref[...]
        h_ref[...] = hs[CHUNK - 1:CHUNK, :]
        gate = z_ref[rows, w:2 * w]
        o_ref[rows, :] = hs * jax.nn.gelu(gate, approximate=True)
        return carry

    lax.fori_loop(0, s // CHUNK, body, 0)


def _rg_lru(zb, conv_w, conv_b, wa, ba, wx, bx, lam):
    b, s, _ = zb.shape
    bd = lambda m: jax.scipy.linalg.block_diag(*[m[i] for i in range(m.shape[0])]).astype(BF16)
    vec = lambda v: v.reshape(1, GROUP_WIDTH)
    args = (conv_w, vec(conv_b), bd(wa), vec(ba), bd(wx), vec(bx), vec(lam))
    full = lambda a: pl.BlockSpec(a.shape, lambda i: (0,) * a.ndim)
    return pl.pallas_call(
        _lru_kernel,
        grid=(b,),
        in_specs=[pl.BlockSpec((None, s, ZB_W), lambda i: (i, 0, 0))] + [full(a) for a in args],
        out_specs=pl.BlockSpec((None, s, GROUP_WIDTH), lambda i: (i, 0, 0)),
        out_shape=jax.ShapeDtypeStruct((b, s, GROUP_WIDTH), F32),
        scratch_shapes=[pltpu.VMEM((1, GROUP_WIDTH), F32)],
        compiler_params=pltpu.CompilerParams(dimension_semantics=("parallel",),
                                             vmem_limit_bytes=VMEM_LIMIT),
        name="rg_lru",
    )(zb, *args)


def _mla_kernel(z_ref, pos_ref, invf_ref, qn_ref, wq_ref, kvn_ref, wk_ref, wv_ref, o_ref,
                q_s, k_s, v_s):
    s = z_ref.shape[0]
    nt = s // ATT_TILE
    scale = (MLA_NOPE + MLA_ROPE) ** -0.5
    lane = lax.broadcasted_iota(jnp.int32, (ATT_TILE, LANES), 1)
    first = (lane >= MLA_NOPE) & (lane < MLA_NOPE + MLA_ROPE // 2)

    def rope(x, cos, sin_signed):
        return x * cos + _lane_block_swap(x, MLA_ROPE // 2, first) * sin_signed

    def project(t, carry):
        rows = pl.ds(pl.multiple_of(t * ATT_TILE, ATT_TILE), ATT_TILE)
        ang = pos_ref[rows, :] * invf_ref[...]
        cos = jnp.cos(ang)
        sin = jnp.sin(ang)
        sin_signed = jnp.where(first, -sin, sin)
        cq = (_rms(z_ref[rows, 0:ZC_KV], MLA_Q_LORA) * qn_ref[...]).astype(BF16)
        ckv = (_rms(z_ref[rows, ZC_KV:ZC_KR], MLA_KV_LORA) * kvn_ref[...]).astype(BF16)
        k_rope = rope(z_ref[rows, ZC_KR:ZC_W], cos, sin_signed)
        v_s[rows, :] = jnp.dot(ckv, wv_ref[...], preferred_element_type=F32).astype(BF16)
        for h in range(HEADS):
            q = jnp.dot(cq, wq_ref[h], preferred_element_type=F32)
            q_s[h, rows, :] = rope(q, cos, sin_signed).astype(BF16)
            k = jnp.dot(ckv, wk_ref[h], preferred_element_type=F32)
            k_s[h, rows, :] = (k + k_rope).astype(BF16)
        return carry

    lax.fori_loop(0, nt, project, 0)

    r_id = lax.broadcasted_iota(jnp.int32, (ATT_TILE, ATT_TILE), 0) // CHUNK
    c_id = lax.broadcasted_iota(jnp.int32, (ATT_TILE, ATT_TILE), 1) // CHUNK
    diag_visible = c_id <= r_id

    def q_tile(i, carry):
        rows = pl.ds(pl.multiple_of(i * ATT_TILE, ATT_TILE), ATT_TILE)

        def kv_tile(j, state, on_diagonal):
            cols = pl.ds(pl.multiple_of(j * ATT_TILE, ATT_TILE), ATT_TILE)
            new_state = []
            for h in range(HEADS):
                m, l, acc = state[h]
                sc = lax.dot_general(q_s[h, rows, :], k_s[h, cols, :], (((1,), (1,)), ((), ())),
                                     preferred_element_type=F32) * scale
                if on_diagonal:
                    sc = jnp.where(diag_visible, sc, -jnp.inf)
                m_new = jnp.maximum(m, jnp.max(sc, axis=-1, keepdims=True))
                p = jnp.exp(sc - m_new)
                alpha = jnp.exp(m - m_new)
                l = alpha * l + jnp.sum(p, axis=-1, keepdims=True)
                v = v_s[cols, h * HEAD_DIM:(h + 1) * HEAD_DIM]
                acc = alpha * acc + jnp.dot(p.astype(BF16), v, preferred_element_type=F32)
                new_state.append((m_new, l, acc))
            return tuple(new_state)

        init = tuple((jnp.full((ATT_TILE, 1), -jnp.inf, F32), jnp.zeros((ATT_TILE, 1), F32),
                      jnp.zeros((ATT_TILE, HEAD_DIM), F32)) for _ in range(HEADS))
        state = lax.fori_loop(0, i, lambda j, st: kv_tile(j, st, False), init)
        state = kv_tile(i, state, True)
        o_ref[rows, :] = jnp.concatenate([acc / l for _, l, acc in state], axis=1)
        return carry

    lax.fori_loop(0, nt, q_tile, 0)


def _mla(zc, pos_f, q_norm, w_uq, kv_norm, w_ukv):
    b, s, _ = zc.shape
    h = HEADS
    half = MLA_ROPE // 2
    inv_freq = jnp.power(ROPE_THETA, -jnp.arange(half, dtype=F32) / half)
    invf = jnp.zeros((1, LANES), F32).at[0, MLA_NOPE:MLA_NOPE + MLA_ROPE].set(jnp.tile(inv_freq, 2))
    qn = jnp.zeros((1, ZC_KV), F32).at[0, :MLA_Q_LORA].set(q_norm)
    kvn = kv_norm.reshape(1, MLA_KV_LORA)
    wq = w_uq.reshape(MLA_Q_LORA, h, MLA_NOPE + MLA_ROPE).transpose(1, 0, 2)
    wq = jnp.pad(wq, ((0, 0), (0, ZC_KV - MLA_Q_LORA), (0, LANES - MLA_NOPE - MLA_ROPE))).astype(BF16)
    wkv = w_ukv.reshape(MLA_KV_LORA, h, MLA_NOPE + HEAD_DIM)
    wk = jnp.pad(wkv[:, :, :MLA_NOPE].transpose(1, 0, 2),
                 ((0, 0), (0, 0), (0, LANES - MLA_NOPE))).astype(BF16)
    wv = wkv[:, :, MLA_NOPE:].reshape(MLA_KV_LORA, h * HEAD_DIM).astype(BF16)
    args = (invf, qn, wq, kvn, wk, wv)
    full = lambda a: pl.BlockSpec(a.shape, lambda i: (0,) * a.ndim)
    return pl.pallas_call(
        _mla_kernel,
        grid=(b,),
        in_specs=[pl.BlockSpec((None, s, ZC_W), lambda i: (i, 0, 0)),
                  pl.BlockSpec((None, s, 1), lambda i: (i, 0, 0))] + [full(a) for a in args],
        out_specs=pl.BlockSpec((None, s, GROUP_WIDTH), lambda i: (i, 0, 0)),
        out_shape=jax.ShapeDtypeStruct((b, s, GROUP_WIDTH), F32),
        scratch_shapes=[pltpu.VMEM((h, s, LANES), BF16), pltpu.VMEM((h, s, LANES), BF16),
                        pltpu.VMEM((s, GROUP_WIDTH), BF16)],
        compiler_params=pltpu.CompilerParams(dimension_semantics=("parallel",),
                                             vmem_limit_bytes=VMEM_LIMIT),
        name="mla",
    )(zc, pos_f, *args)


def _dot_exact(a, b, parts):
    f32_is_lhs = a.dtype == F32
    r = a if f32_is_lhs else b
    terms = []
    for _ in range(parts):
        p = r.astype(BF16)
        terms.append(jnp.dot(p, b, preferred_element_type=F32) if f32_is_lhs
                     else jnp.dot(a, p, preferred_element_type=F32))
        r = r - p.astype(F32)
    acc = terms[-1]
    for t in terms[-2::-1]:
        acc = acc + t
    return acc


def _gdn_kernel(z_ref, cw_ref, alog_ref, dtb_ref, ng_ref, o_ref,
                u_s, wq_s, kd_s, attn_s, gl_s, state_ref):
    s = z_ref.shape[0]
    gw = GROUP_WIDTH
    g_rows = GDN_ROWS
    per = g_rows // CHUNK
    ri = lax.broadcasted_iota(jnp.int32, (g_rows, g_rows), 0)
    ci = lax.broadcasted_iota(jnp.int32, (g_rows, g_rows), 1)
    same_block = (ri // CHUNK) == (ci // CHUNK)
    causal = same_block & (ri >= ci)
    strict = same_block & (ri > ci)
    tri = causal.astype(BF16)
    ones_bd = same_block.astype(BF16)
    er = lax.broadcasted_iota(jnp.int32, (LANES, gw), 0)
    ec = lax.broadcasted_iota(jnp.int32, (LANES, gw), 1) // HEAD_DIM
    expand_beta = (er == ec).astype(BF16)
    expand_alpha = (er == ec + HEADS).astype(BF16)

    def precompute(gi, carry):
        rows = pl.ds(pl.multiple_of(gi * g_rows, g_rows), g_rows)
        qkv = _silu(_causal_conv_rows(z_ref, gi, g_rows, 0, 3 * gw, cw_ref[...]))
        q = qkv[:, 0:gw]
        k = qkv[:, gw:2 * gw]
        v = qkv[:, 2 * gw:3 * gw]
        qn = q * lax.rsqrt(_dot_exact(q * q, ones_bd, 2) + EPS) * (HEAD_DIM ** -0.5)
        kn = k * lax.rsqrt(_dot_exact(k * k, ones_bd, 2) + EPS)
        small = z_ref[rows, 4 * gw:4 * gw + LANES]
        beta_t = jax.nn.sigmoid(small)
        g_t = -jnp.exp(alog_ref[...]) * _softplus(small + dtb_ref[...])
        beta_w = _dot_exact(beta_t, expand_beta, 3)
        g_w = _dot_exact(g_t, expand_alpha, 3)
        gc_w = _dot_exact(tri, g_w, 3)
        gl_w = _dot_exact(ones_bd, g_w, 3)
        gc_t = _dot_exact(tri, g_t, 3)
        gc_tt = gc_t.T
        egc_w = jnp.exp(gc_w)
        kb_w = kn * beta_w
        vb_w = v * beta_w
        kbe_w = kb_w * egc_w
        us, ws = [], []
        for h in range(HEADS):
            sl = slice(h * HEAD_DIM, (h + 1) * HEAD_DIM)
            gcol = jnp.broadcast_to(gc_t[:, HEADS + h:HEADS + h + 1], (g_rows, g_rows))
            grow = jnp.broadcast_to(gc_tt[HEADS + h:HEADS + h + 1, :], (g_rows, g_rows))
            decay = jnp.exp(jnp.where(causal, gcol - grow, -jnp.inf))
            kh = kn[:, sl].astype(BF16)
            lower = jnp.where(strict, _dot_nt(kb_w[:, sl], kh) * decay, 0.0).astype(BF16)
            attn = jnp.where(causal, _dot_nt(qn[:, sl], kh) * decay, 0.0)
            attn_s[h, rows, :] = attn.astype(BF16)
            sol = jnp.concatenate([vb_w[:, sl], kbe_w[:, sl]], axis=1)
            sol = sol - _dot(lower, sol)
            p = lower
            m = 2
            while m < CHUNK:
                p = _dot(p, p).astype(BF16)
                sol = sol + _dot(p, sol)
                m *= 2
            us.append(sol[:, :HEAD_DIM])
            ws.append(sol[:, HEAD_DIM:])
        u_s[rows, :] = jnp.concatenate(us, axis=1)
        w_all = jnp.concatenate(ws, axis=1).astype(BF16)
        qd_w = (qn * egc_w).astype(BF16)
        kd_s[rows, :] = (kn * jnp.exp(gl_w - gc_w)).astype(BF16)
        for c in range(per):
            blk = slice(c * CHUNK, (c + 1) * CHUNK)
            wq_s[gi * per + c, 0:CHUNK, :] = w_all[blk]
            wq_s[gi * per + c, CHUNK:2 * CHUNK, :] = qd_w[blk]
            gl_s[gi * per + c] = gl_w[c * CHUNK:c * CHUNK + 1, :]
        return carry

    lax.fori_loop(0, s // g_rows, precompute, 0)

    state_ref[...] = jnp.zeros_like(state_ref)

    def recur(gi, carry):
        for c in range(per):
            n = gi * per + c
            rows = _chunk_rows(n)
            st = state_ref[...]
            r1 = _dot(wq_s[n], st)
            v_new = (u_s[rows, :] - r1[0:CHUNK]).astype(BF16)
            parts = []
            for h in range(HEADS):
                sl = slice(h * HEAD_DIM, (h + 1) * HEAD_DIM)
                a = attn_s[h, rows, c * CHUNK:(c + 1) * CHUNK]
                parts.append(jnp.dot(a, v_new[:, sl], preferred_element_type=F32))
            o = r1[CHUNK:2 * CHUNK] + jnp.concatenate(parts, axis=1)
            ds = lax.dot_general(kd_s[rows, :], v_new, (((0,), (0,)), ((), ())),
                                 preferred_element_type=F32)
            state_ref[...] = st * jnp.exp(gl_s[n]) + jnp.where(same_block, ds, 0.0)
            ms = _dot_exact(o * o, ones_bd, 2) * (1.0 / HEAD_DIM)
            gate = z_ref[rows, 3 * gw:4 * gw]
            o_ref[rows, :] = o * lax.rsqrt(ms + EPS) * ng_ref[...] * _silu(gate)
        return carry

    lax.fori_loop(0, s // g_rows, recur, 0)


def _gated_deltanet(zd, conv_w, a_log, dt_bias, norm_g):
    b, s, _ = zd.shape
    lane_vec = lambda p: jnp.zeros((1, LANES), F32).at[0, HEADS:2 * HEADS].set(p)
    ng = jnp.tile(norm_g, HEADS)[None, :]
    args = (conv_w, lane_vec(a_log), lane_vec(dt_bias), ng)
    full = lambda a: pl.BlockSpec(a.shape, lambda i: (0,) * a.ndim)
    return pl.pallas_call(
        _gdn_kernel,
        grid=(b,),
        in_specs=[pl.BlockSpec((None, s, ZD_W), lambda i: (i, 0, 0))] + [full(a) for a in args],
        out_specs=pl.BlockSpec((None, s, GROUP_WIDTH), lambda i: (i, 0, 0)),
        out_shape=jax.ShapeDtypeStruct((b, s, GROUP_WIDTH), F32),
        scratch_shapes=[pltpu.VMEM((s, GROUP_WIDTH), F32),
                        pltpu.VMEM((s // CHUNK, 2 * CHUNK, GROUP_WIDTH), BF16),
                        pltpu.VMEM((s, GROUP_WIDTH), BF16),
                        pltpu.VMEM((HEADS, s, GDN_ROWS), BF16),
                        pltpu.VMEM((s // CHUNK, 1, GROUP_WIDTH), F32),
                        pltpu.VMEM((GROUP_WIDTH, GROUP_WIDTH), F32)],
        compiler_params=pltpu.CompilerParams(dimension_semantics=("parallel",),
                                             vmem_limit_bytes=VMEM_LIMIT),
        name="gated_deltanet",
    )(zd, *args)


def _mix_out_kernel(x_ref, ya_ref, yb_ref, yc_ref, yd_ref, wo_ref, g_ref, wq_ref, x1_ref, q_ref):
    acc = x_ref[...]
    for gi, y_ref in enumerate((ya_ref, yb_ref, yc_ref, yd_ref)):
        acc = acc + jnp.dot(y_ref[...].astype(BF16), wo_ref[gi * GROUP_WIDTH:(gi + 1) * GROUP_WIDTH, :],
                            preferred_element_type=F32)
    x1_ref[...] = acc
    hq = (_rms(acc, D_MODEL) * g_ref[...]).astype(BF16)
    q_ref[...] = jnp.dot(hq, wq_ref[...], preferred_element_type=F32).astype(BF16)


def _mix_out(x2d, ys, w_out, gain, wq):
    t = x2d.shape[0]
    row = lambda w: pl.BlockSpec((ROW_TILE, w), lambda i: (i, 0))
    full = lambda a: pl.BlockSpec(a.shape, lambda i: (0,) * a.ndim)
    return pl.pallas_call(
        _mix_out_kernel,
        grid=(t // ROW_TILE,),
        in_specs=[row(D_MODEL)] + [row(GROUP_WIDTH)] * 4 + [full(w_out), full(gain), full(wq)],
        out_specs=[row(D_MODEL), row(D_MODEL)],
        out_shape=[jax.ShapeDtypeStruct((t, D_MODEL), F32), jax.ShapeDtypeStruct((t, D_MODEL), BF16)],
        compiler_params=pltpu.CompilerParams(dimension_semantics=("parallel",),
                                             vmem_limit_bytes=VMEM_LIMIT),
        name="mix_out",
    )(x2d, *ys, w_out, gain, wq)


def _mem_kv_kernel(mem_ref, g_ref, wkv_ref, k_ref, v_ref):
    m = (_rms(mem_ref[...], D_MODEL) * g_ref[...]).astype(BF16)
    k_ref[...] = jnp.dot(m, wkv_ref[:, :D_MODEL], preferred_element_type=F32).astype(BF16)
    v_ref[...] = jnp.dot(m, wkv_ref[:, D_MODEL:], preferred_element_type=F32).astype(BF16)


def _mem_kv(mem, gain, wkv):
    b, m, _ = mem.shape
    full = lambda a: pl.BlockSpec(a.shape, lambda i: (0,) * a.ndim)
    blk = pl.BlockSpec((None, m, D_MODEL), lambda i: (i, 0, 0))
    return pl.pallas_call(
        _mem_kv_kernel,
        grid=(b,),
        in_specs=[blk, full(gain), full(wkv)],
        out_specs=[blk, blk],
        out_shape=[jax.ShapeDtypeStruct((b, m, D_MODEL), BF16)] * 2,
        compiler_params=pltpu.CompilerParams(dimension_semantics=("parallel",),
                                             vmem_limit_bytes=VMEM_LIMIT),
        name="mem_kv",
    )(mem, gain, wkv)


def _xattn_mlp_kernel(x_ref, q_ref, k_ref, v_ref, wo_ref, g_ref, w1_ref, w2_ref, fg_ref, o_ref,
                      *, final_norm):
    x = x_ref[...]
    q = q_ref[...]
    heads = []
    for h in range(XA_HEADS):
        sl = slice(h * XA_HEAD_DIM, (h + 1) * XA_HEAD_DIM)
        sc = lax.dot_general(q[:, sl], k_ref[:, sl], (((1,), (1,)), ((), ())),
                             preferred_element_type=F32) * (XA_HEAD_DIM ** -0.5)
        e = jnp.exp(sc - jnp.max(sc, axis=-1, keepdims=True))
        p = e / jnp.sum(e, axis=-1, keepdims=True)
        heads.append(jnp.dot(p.astype(BF16), v_ref[:, sl], preferred_element_type=F32).astype(BF16))
    o = jnp.concatenate(heads, axis=1)
    x = x + jnp.dot(o, wo_ref[...], preferred_element_type=F32)
    hm = (_rms(x, D_MODEL) * g_ref[...]).astype(BF16)
    for c in range(D_FF // FF_SLAB):
        sl = slice(c * FF_SLAB, (c + 1) * FF_SLAB)
        a = jnp.maximum(jnp.dot(hm, w1_ref[:, sl], preferred_element_type=F32), 0.0)
        x = x + jnp.dot((a * a).astype(BF16), w2_ref[sl, :], preferred_element_type=F32)
    if final_norm:
        x = _rms(x, D_MODEL) * fg_ref[...]
    o_ref[...] = x


def _xattn_mlp(x1, q, k, v, wo, gain, w1, w2, final_gain, final_norm):
    b, s, _ = x1.shape
    m = k.shape[1]
    row = pl.BlockSpec((None, ROW_TILE, D_MODEL), lambda i, j: (i, j, 0))
    mem = pl.BlockSpec((None, m, D_MODEL), lambda i, j: (i, 0, 0))
    const = lambda a: pl.BlockSpec(a.shape, lambda i, j: (0,) * a.ndim, pipeline_mode=pl.Buffered(1))
    return pl.pallas_call(
        functools.partial(_xattn_mlp_kernel, final_norm=final_norm),
        grid=(b, s // ROW_TILE),
        in_specs=[row, row, mem, mem, const(wo), const(gain), const(w1), const(w2), const(final_gain)],
        out_specs=row,
        out_shape=jax.ShapeDtypeStruct((b, s, D_MODEL), F32),
        compiler_params=pltpu.CompilerParams(dimension_semantics=("parallel", "parallel"),
                                             vmem_limit_bytes=VMEM_LIMIT),
        name="xattn_mlp",
    )(x1, q, k, v, wo, gain, w1, w2, final_gain)


def _pack_w_in(w):
    d = w.shape[0]
    zeros = lambda n: jnp.zeros((d, n), w.dtype)
    c0 = ZA_W + ZB_W
    cq = w[:, c0:c0 + MLA_Q_LORA]
    ckv = w[:, c0 + MLA_Q_LORA:c0 + MLA_Q_LORA + MLA_KV_LORA]
    kr = w[:, c0 + MLA_Q_LORA + MLA_KV_LORA:c0 + MLA_Q_LORA + MLA_KV_LORA + MLA_ROPE]
    d0 = c0 + MLA_Q_LORA + MLA_KV_LORA + MLA_ROPE
    rest = w[:, d0:]
    packed = jnp.concatenate([
        w[:, :c0],
        cq, zeros(ZC_KV - MLA_Q_LORA), ckv, zeros(MLA_NOPE), kr, zeros(LANES - MLA_NOPE - MLA_ROPE),
        rest, zeros(ZD_W - rest.shape[1]),
    ], axis=1)
    assert packed.shape[1] == Z_W
    return packed.astype(BF16)


def kernel(x, mem, positions, norm_mix, w_in, ret_gn, lru_conv_w, lru_conv_b, lru_wa, lru_ba, lru_wx, lru_bx, lru_lambda, mla_q_norm, mla_w_uq, mla_kv_norm, mla_w_ukv, gdn_conv_w, gdn_a_log, gdn_dt_bias, gdn_norm, w_out, norm_xattn, norm_mem, xa_wq, xa_wkv, xa_wo, norm_mlp, mlp_w1, mlp_w2, final_norm):
    b, s, d = x.shape
    depth = w_in.shape[0]
    pos_f = positions.astype(F32)[..., None]
    vec = lambda v: v.reshape(1, -1)
    for l in range(depth):
        za, zb, zc, zd = _in_proj(x.reshape(b * s, d), vec(norm_mix[l]), _pack_w_in(w_in[l]))
        seq = lambda z: z.reshape(b, s, z.shape[-1])
        y_a = _retention(seq(za), pos_f, ret_gn[l])
        y_b = _rg_lru(seq(zb), lru_conv_w[l], lru_conv_b[l], lru_wa[l], lru_ba[l], lru_wx[l],
                      lru_bx[l], lru_lambda[l])
        y_c = _mla(seq(zc), pos_f, mla_q_norm[l], mla_w_uq[l], mla_kv_norm[l], mla_w_ukv[l])
        y_d = _gated_deltanet(seq(zd), gdn_conv_w[l], gdn_a_log[l], gdn_dt_bias[l], gdn_norm[l])
        ys = [y.reshape(b * s, GROUP_WIDTH) for y in (y_a, y_b, y_c, y_d)]
        x1, q = _mix_out(x.reshape(b * s, d), ys, w_out[l].astype(BF16), vec(norm_xattn[l]),
                         xa_wq[l].astype(BF16))
        k, v = _mem_kv(mem, vec(norm_mem[l]), xa_wkv[l].astype(BF16))
        x = _xattn_mlp(x1.reshape(b, s, d), q.reshape(b, s, d), k, v, xa_wo[l].astype(BF16),
                       vec(norm_mlp[l]), mlp_w1[l].astype(BF16), mlp_w2[l].astype(BF16),
                       vec(final_norm), l == depth - 1)
    return x
```

```python
import functools

import jax
import jax.numpy as jnp
from jax import lax
from jax.experimental import pallas as pl
from jax.experimental.pallas import tpu as pltpu

F32 = jnp.float32
BF16 = jnp.bfloat16

D_MODEL = 1024
CHUNK = 64
HEADS = 4
HEAD_DIM = 64
GROUP_WIDTH = HEADS * HEAD_DIM
EPS = 1e-6
ROPE_THETA = 10000.0
LRU_C = 8.0
MLA_Q_LORA = 192
MLA_KV_LORA = 128
MLA_ROPE = 32
MLA_NOPE = 64
XA_HEADS = 4
XA_HEAD_DIM = D_MODEL // XA_HEADS
D_FF = 4 * D_MODEL
FF_SLAB = 1024
ATT_TILE = 256
ROPE_TILE = 256
assert ATT_TILE == ROPE_TILE
GDN_ROWS = 256
SOLVE_BASE = 16
assert GDN_ROWS == GROUP_WIDTH and HEAD_DIM == CHUNK
ROW_TILE = 512
SUBLANES = 8
LANES = 128
VMEM_LIMIT = 56 * 1024 * 1024

ZA_W = 4 * GROUP_WIDTH
ZB_W = 2 * GROUP_WIDTH
ZC_W = 512
ZD_W = 1152
ZC_KV = 256
ZC_KR = 384
Z_W = ZA_W + ZB_W + ZC_W + ZD_W


def _dot(a, b):
    return jnp.dot(a.astype(BF16), b.astype(BF16), preferred_element_type=F32)


def _dot_nt(a, b):
    return lax.dot_general(a.astype(BF16), b.astype(BF16), (((1,), (1,)), ((), ())),
                           preferred_element_type=F32)


def _dot_tn(a, b):
    return lax.dot_general(a.astype(BF16), b.astype(BF16), (((0,), (0,)), ((), ())),
                           preferred_element_type=F32)


def _split3(x):
    x1 = x.astype(BF16)
    r1 = x - x1.astype(F32)
    x2 = r1.astype(BF16)
    x3 = (r1 - x2.astype(F32)).astype(BF16)
    return x1, x2, x3


def _dot3(a, b):
    a1 = a.astype(BF16)
    a2 = (a - a1.astype(F32)).astype(BF16)
    b1 = b.astype(BF16)
    b2 = (b - b1.astype(F32)).astype(BF16)
    d = functools.partial(jnp.dot, preferred_element_type=F32)
    return d(a1, b1) + (d(a1, b2) + d(a2, b1))


def _silu(x):
    return x * jax.nn.sigmoid(x)


def _softplus(x):
    return jnp.maximum(x, 0.0) + jnp.log1p(jnp.exp(-jnp.abs(x)))


def _expm1(y):
    e = jnp.exp(y)
    near = (y > -0.5) & (e != 1.0)
    safe_log = jnp.where(near, jnp.log(e), 1.0)
    return jnp.where(near, (e - 1.0) * y / safe_log, jnp.where(e == 1.0, y, e - 1.0))


def _rms(x, n):
    return x * lax.rsqrt(jnp.sum(x * x, axis=-1, keepdims=True) * (1.0 / n) + EPS)


def _chunk_rows(n):
    return pl.ds(pl.multiple_of(n * CHUNK, CHUNK), CHUNK)


def _causal_conv_rows(z_ref, n, nrows, lo, hi, w):
    cur = z_ref[pl.ds(pl.multiple_of(n * nrows, nrows), nrows), lo:hi]
    prev_start = pl.multiple_of(jnp.maximum(n * nrows - SUBLANES, 0), SUBLANES)
    prev = z_ref[pl.ds(prev_start, SUBLANES), lo:hi]
    prev = jnp.where(n > 0, prev, 0.0)
    tile = jnp.concatenate([prev, cur], axis=0)
    acc = cur * w[3:4, :]
    for k in (1, 2, 3):
        acc = acc + pltpu.roll(tile, k, 0)[SUBLANES:, :] * w[3 - k:4 - k, :]
    return acc


def _lane_block_swap(x, half, first):
    return jnp.where(first, pltpu.roll(x, LANES - half, 1), pltpu.roll(x, half, 1))


def _in_proj_kernel(x_ref, g_ref, w_ref, za_ref, zb_ref, zc_ref, zd_ref):
    x = x_ref[...]
    h = (_rms(x, D_MODEL) * g_ref[...]).astype(BF16)
    lo = 0
    for ref in (za_ref, zb_ref, zc_ref, zd_ref):
        hi = lo + ref.shape[-1]
        ref[...] = jnp.dot(h, w_ref[:, lo:hi], preferred_element_type=F32)
        lo = hi


def _in_proj(x2d, gain, w_packed):
    t = x2d.shape[0]
    row = lambda w: pl.BlockSpec((ROW_TILE, w), lambda i: (i, 0))
    full = lambda a: pl.BlockSpec(a.shape, lambda i: (0,) * a.ndim)
    return pl.pallas_call(
        _in_proj_kernel,
        grid=(t // ROW_TILE,),
        in_specs=[row(D_MODEL), full(gain), full(w_packed)],
        out_specs=[row(ZA_W), row(ZB_W), row(ZC_W), row(ZD_W)],
        out_shape=[jax.ShapeDtypeStruct((t, w), F32) for w in (ZA_W, ZB_W, ZC_W, ZD_W)],
        compiler_params=pltpu.CompilerParams(dimension_semantics=("parallel",),
                                             vmem_limit_bytes=VMEM_LIMIT),
        name="in_proj",
    )(x2d, gain, w_packed)


def _compact_positions(pos_f, n_freq):
    b, s = pos_f.shape
    nb = LANES // n_freq
    r = ROPE_TILE // nb
    p = pos_f.reshape(b, s // ROPE_TILE, nb, r).transpose(0, 1, 3, 2)
    return jnp.repeat(p, n_freq, axis=-1)


def _rope_expanders(n_freq, width, rope_lo, rope_hi):
    nb = LANES // n_freq
    src = jnp.arange(LANES)
    lane = jnp.arange(width)
    on_rope = (lane >= rope_lo) & (lane < rope_hi)
    hit = (src[:, None] % n_freq == lane[None, :] % n_freq) & on_rope[None, :]
    return jnp.stack([(hit & (src[:, None] // n_freq == r)) for r in range(nb)]).astype(BF16)


def _expand_rope_table(t_c, e_ref):
    return jnp.concatenate([_dot_exact(t_c, e_ref[r], 3) for r in range(e_ref.shape[0])], axis=0)


def _retention_kernel(z_ref, posc_ref, invf_ref, e_ref, dmat_ref, xi_ref, zeta_ref, cd_ref, gn_ref,
                      o_ref, state_ref):
    s = z_ref.shape[0]
    gw = GROUP_WIDTH
    n_rows = ROPE_TILE
    lane = lax.broadcasted_iota(jnp.int32, (n_rows, LANES), 1)
    first = (lane % HEAD_DIM) < (HEAD_DIM // 2)
    lane_wide = lax.broadcasted_iota(jnp.int32, (n_rows, gw), 1)
    first_wide = (lane_wide % HEAD_DIM) < (HEAD_DIM // 2)
    ri = lax.broadcasted_iota(jnp.int32, (gw, gw), 0)
    ci = lax.broadcasted_iota(jnp.int32, (gw, gw), 1)
    same_head = (ri // HEAD_DIM) == (ci // HEAD_DIM)
    ones_bd = same_head.astype(BF16)
    state_ref[...] = jnp.zeros_like(state_ref)

    def rope(x, cos, sin_signed):
        parts = []
        for c in range(gw // LANES):
            sl = slice(c * LANES, (c + 1) * LANES)
            xs = _lane_block_swap(x[:, sl], HEAD_DIM // 2, first)
            parts.append(x[:, sl] * cos[:, sl] + xs * sin_signed[:, sl])
        return jnp.concatenate(parts, axis=1)

    def body(g, carry):
        rows = pl.ds(pl.multiple_of(g * n_rows, n_rows), n_rows)
        ang = posc_ref[g] * invf_ref[...]
        cos = _expand_rope_table(jnp.cos(ang), e_ref)
        sin = _expand_rope_table(jnp.sin(ang), e_ref)
        sin_signed = jnp.where(first_wide, -sin, sin)
        q = (rope(z_ref[rows, 0:gw], cos, sin_signed) * (HEAD_DIM ** -0.5)).astype(BF16)
        k = rope(z_ref[rows, gw:2 * gw], cos, sin_signed)
        v = z_ref[rows, 2 * gw:3 * gw].astype(BF16)
        kz = (k * zeta_ref[...]).astype(BF16)
        k = k.astype(BF16)
        heads = [slice(h * HEAD_DIM, (h + 1) * HEAD_DIM) for h in range(HEADS)]
        blocks = [slice(c * CHUNK, (c + 1) * CHUNK) for c in range(n_rows // CHUNK)]
        scores = [_dot_nt(q[:, sl], k[:, sl]) for sl in heads]
        kvs = [_dot_tn(kz[blk], v[blk]) for blk in blocks]
        intra = [_dot(scores[h] * dmat_ref[h], v[:, sl])
                 for h, sl in enumerate(heads)]
        st = state_ref[...]
        cross = []
        for c, blk in enumerate(blocks):
            cross.append(_dot(q[blk], st))
            st = st * cd_ref[...] + jnp.where(same_head, kvs[c], 0.0)
        state_ref[...] = st
        o = jnp.concatenate(intra, axis=1) + jnp.concatenate(cross, axis=0) * xi_ref[...]
        mu = _dot_exact(o, ones_bd, 2) * (1.0 / HEAD_DIM)
        d = o - mu
        var = _dot_exact(d * d, ones_bd, 2) * (1.0 / HEAD_DIM)
        gate = z_ref[rows, 3 * gw:4 * gw]
        o_ref[rows, :] = d * lax.rsqrt(var + EPS) * gn_ref[...] * _silu(gate)
        return carry

    lax.fori_loop(0, s // n_rows, body, 0)


def _retention(za, pos_f, ret_gn):
    b, s, _ = za.shape
    h = HEADS
    half = HEAD_DIM // 2
    per = ROPE_TILE // CHUNK
    inv_freq = jnp.power(ROPE_THETA, -jnp.arange(half, dtype=F32) / half)
    invf = jnp.tile(inv_freq, LANES // half)[None, :]
    posc = _compact_positions(pos_f, half)
    expand = _rope_expanders(half, GROUP_WIDTH, 0, GROUP_WIDTH)
    log_gamma = jnp.log1p(-jnp.exp2(-5.0 - jnp.arange(h, dtype=F32)))
    idx = jnp.arange(CHUNK, dtype=F32)
    dmat = jnp.exp(jnp.abs(idx[:, None] - idx[None, :])[None] * log_gamma[:, None, None])
    dmat = jnp.stack([jnp.kron(jnp.eye(per, dtype=F32), dmat[i]) for i in range(h)])
    xi = jnp.repeat(jnp.exp((idx[:, None] + 1.0) * log_gamma), HEAD_DIM, axis=1)
    zeta = jnp.repeat(jnp.exp((CHUNK - 1.0 - idx)[:, None] * log_gamma), HEAD_DIM, axis=1)
    xi = jnp.tile(xi, (per, 1))
    zeta = jnp.tile(zeta, (per, 1))
    cd = jnp.repeat(jnp.exp(CHUNK * log_gamma), HEAD_DIM)[None, :]
    gn = ret_gn.reshape(1, GROUP_WIDTH)
    consts = (invf, expand, dmat, xi, zeta, cd, gn)
    full = lambda a: pl.BlockSpec(a.shape, lambda i: (0,) * a.ndim)
    return pl.pallas_call(
        _retention_kernel,
        grid=(b,),
        in_specs=[pl.BlockSpec((None, s, ZA_W), lambda i: (i, 0, 0)),
                  pl.BlockSpec((None,) + posc.shape[1:], lambda i: (i, 0, 0, 0))]
                 + [full(a) for a in consts],
        out_specs=pl.BlockSpec((None, s, GROUP_WIDTH), lambda i: (i, 0, 0)),
        out_shape=jax.ShapeDtypeStruct((b, s, GROUP_WIDTH), F32),
        scratch_shapes=[pltpu.VMEM((GROUP_WIDTH, GROUP_WIDTH), F32)],
        compiler_params=pltpu.CompilerParams(dimension_semantics=("parallel",),
                                             vmem_limit_bytes=VMEM_LIMIT),
        name="retention",
    )(za, posc, *consts)


def _lru_kernel(z_ref, cw_ref, cb_ref, wa_ref, ba_ref, wx_ref, bx_ref, lam_ref, o_ref, h_ref):
    s = z_ref.shape[0]
    w = GROUP_WIDTH
    row = lax.broadcasted_iota(jnp.int32, (CHUNK, w), 0)
    h_ref[...] = jnp.zeros_like(h_ref)

    def body(n, carry):
        rows = _chunk_rows(n)
        xc = _causal_conv_rows(z_ref, n, CHUNK, 0, w, cw_ref[...]) + cb_ref[...]
        r = jax.nn.sigmoid(_dot(xc, wa_ref[...]) + ba_ref[...])
        i = jax.nn.sigmoid(_dot(xc, wx_ref[...]) + bx_ref[...])
        log_a = -LRU_C * r * _softplus(-lam_ref[...])
        a = jnp.exp(log_a)
        u = jnp.sqrt(-_expm1(2.0 * log_a)) * (i * xc)
        d = 1
        while d < CHUNK:
            a_sh = jnp.where(row >= d, pltpu.roll(a, d, 0), 1.0)
            u_sh = jnp.where(row >= d, pltpu.roll(u, d, 0), 0.0)
            u = a * u_sh + u
            a = a * a_sh
            d *= 2
        hs = u + a * h_ref[...]
        h_ref[...] = hs[CHUNK - 1:CHUNK, :]
        gate = z_ref[rows, w:2 * w]
        o_ref[rows, :] = hs * jax.nn.gelu(gate, approximate=True)
        return carry

    lax.fori_loop(0, s // CHUNK, body, 0)


def _rg_lru(zb, conv_w, conv_b, wa, ba, wx, bx, lam):
    b, s, _ = zb.shape
    bd = lambda m: jax.scipy.linalg.block_diag(*[m[i] for i in range(m.shape[0])]).astype(BF16)
    vec = lambda v: v.reshape(1, GROUP_WIDTH)
    args = (conv_w, vec(conv_b), bd(wa), vec(ba), bd(wx), vec(bx), vec(lam))
    full = lambda a: pl.BlockSpec(a.shape, lambda i: (0,) * a.ndim)
    return pl.pallas_call(
        _lru_kernel,
        grid=(b,),
        in_specs=[pl.BlockSpec((None, s, ZB_W), lambda i: (i, 0, 0))] + [full(a) for a in args],
        out_specs=pl.BlockSpec((None, s, GROUP_WIDTH), lambda i: (i, 0, 0)),
        out_shape=jax.ShapeDtypeStruct((b, s, GROUP_WIDTH), F32),
        scratch_shapes=[pltpu.VMEM((1, GROUP_WIDTH), F32)],
        compiler_params=pltpu.CompilerParams(dimension_semantics=("parallel",),
                                             vmem_limit_bytes=VMEM_LIMIT),
        name="rg_lru",
    )(zb, *args)


def _mla_kernel(z_ref, posc_ref, invf_ref, e_ref, base_ref, qn_ref, wq_ref, kvn_ref, wk_ref, wvt_ref,
                o_ref, q_s, k_s, vt_s):
    s = z_ref.shape[0]
    nt = s // ATT_TILE
    scale = (MLA_NOPE + MLA_ROPE) ** -0.5
    lane = lax.broadcasted_iota(jnp.int32, (ATT_TILE, LANES), 1)
    first = (lane >= MLA_NOPE) & (lane < MLA_NOPE + MLA_ROPE // 2)

    def rope(x, cos, sin_signed):
        return x * cos + _lane_block_swap(x, MLA_ROPE // 2, first) * sin_signed

    def project(t, carry):
        rows = pl.ds(pl.multiple_of(t * ATT_TILE, ATT_TILE), ATT_TILE)
        ang = posc_ref[t] * invf_ref[...]
        cos = _expand_rope_table(jnp.cos(ang), e_ref) + base_ref[...]
        sin = _expand_rope_table(jnp.sin(ang), e_ref)
        sin_signed = jnp.where(first, -sin, sin)
        cq = (_rms(z_ref[rows, 0:ZC_KV], MLA_Q_LORA) * qn_ref[...]).astype(BF16)
        ckv_f = _rms(z_ref[rows, ZC_KV:ZC_KR], MLA_KV_LORA) * kvn_ref[...]
        ckv = ckv_f.astype(BF16)
        k_rope = rope(z_ref[rows, ZC_KR:ZC_W], cos, sin_signed)
        vt_s[:, rows] = jnp.dot(wvt_ref[...], ckv_f.T.astype(BF16),
                                preferred_element_type=F32).astype(BF16)
        for h in range(HEADS):
            q = jnp.dot(cq, wq_ref[h], preferred_element_type=F32)
            q_s[h, rows, :] = rope(q, cos, sin_signed).astype(BF16)
            k = jnp.dot(ckv, wk_ref[h], preferred_element_type=F32)
            k_s[h, rows, :] = (k + k_rope).astype(BF16)
        return carry

    lax.fori_loop(0, nt, project, 0)

    key_id = lax.broadcasted_iota(jnp.int32, (ATT_TILE, ATT_TILE), 0) // CHUNK
    qry_id = lax.broadcasted_iota(jnp.int32, (ATT_TILE, ATT_TILE), 1) // CHUNK
    diag_visible = key_id <= qry_id

    def q_tile(i, carry):
        rows = pl.ds(pl.multiple_of(i * ATT_TILE, ATT_TILE), ATT_TILE)

        def kv_tile(j, state, on_diagonal):
            cols = pl.ds(pl.multiple_of(j * ATT_TILE, ATT_TILE), ATT_TILE)
            scores = [lax.dot_general(k_s[h, cols, :], q_s[h, rows, :], (((1,), (1,)), ((), ())),
                                      preferred_element_type=F32) for h in range(HEADS)]
            new_state = []
            for h in range(HEADS):
                m, l, acc = state[h]
                sc = scores[h] * scale
                if on_diagonal:
                    sc = jnp.where(diag_visible, sc, -jnp.inf)
                m_new = jnp.maximum(m, jnp.max(sc, axis=0, keepdims=True))
                p = jnp.exp(sc - m_new)
                alpha = jnp.exp(m - m_new)
                l = alpha * l + jnp.sum(p, axis=0, keepdims=True)
                vt = vt_s[h * HEAD_DIM:(h + 1) * HEAD_DIM, cols]
                acc = alpha * acc + jnp.dot(vt, p.astype(BF16), preferred_element_type=F32)
                new_state.append((m_new, l, acc))
            return tuple(new_state)

        init = tuple((jnp.full((1, ATT_TILE), -jnp.inf, F32), jnp.zeros((1, ATT_TILE), F32),
                      jnp.zeros((HEAD_DIM, ATT_TILE), F32)) for _ in range(HEADS))
        state = lax.fori_loop(0, i, lambda j, st: kv_tile(j, st, False), init)
        state = kv_tile(i, state, True)
        o_t = jnp.concatenate([acc / l for _, l, acc in state], axis=0)
        o_ref[rows, :] = o_t.T
        return carry

    lax.fori_loop(0, nt, q_tile, 0)


def _mla(zc, pos_f, q_norm, w_uq, kv_norm, w_ukv):
    b, s, _ = zc.shape
    h = HEADS
    half = MLA_ROPE // 2
    inv_freq = jnp.power(ROPE_THETA, -jnp.arange(half, dtype=F32) / half)
    invf = jnp.tile(inv_freq, LANES // half)[None, :]
    posc = _compact_positions(pos_f, half)
    expand = _rope_expanders(half, LANES, MLA_NOPE, MLA_NOPE + MLA_ROPE)
    lane = jnp.arange(LANES)
    base = ((lane < MLA_NOPE) | (lane >= MLA_NOPE + MLA_ROPE)).astype(F32)[None, :]
    qn = jnp.zeros((1, ZC_KV), F32).at[0, :MLA_Q_LORA].set(q_norm)
    kvn = kv_norm.reshape(1, MLA_KV_LORA)
    wq = w_uq.reshape(MLA_Q_LORA, h, MLA_NOPE + MLA_ROPE).transpose(1, 0, 2)
    wq = jnp.pad(wq, ((0, 0), (0, ZC_KV - MLA_Q_LORA), (0, LANES - MLA_NOPE - MLA_ROPE))).astype(BF16)
    wkv = w_ukv.reshape(MLA_KV_LORA, h, MLA_NOPE + HEAD_DIM)
    wk = jnp.pad(wkv[:, :, :MLA_NOPE].transpose(1, 0, 2),
                 ((0, 0), (0, 0), (0, LANES - MLA_NOPE))).astype(BF16)
    wvt = wkv[:, :, MLA_NOPE:].reshape(MLA_KV_LORA, h * HEAD_DIM).T.astype(BF16)
    args = (invf, expand, base, qn, wq, kvn, wk, wvt)
    full = lambda a: pl.BlockSpec(a.shape, lambda i: (0,) * a.ndim)
    return pl.pallas_call(
        _mla_kernel,
        grid=(b,),
        in_specs=[pl.BlockSpec((None, s, ZC_W), lambda i: (i, 0, 0)),
                  pl.BlockSpec((None,) + posc.shape[1:], lambda i: (i, 0, 0, 0))]
                 + [full(a) for a in args],
        out_specs=pl.BlockSpec((None, s, GROUP_WIDTH), lambda i: (i, 0, 0)),
        out_shape=jax.ShapeDtypeStruct((b, s, GROUP_WIDTH), F32),
        scratch_shapes=[pltpu.VMEM((h, s, LANES), BF16), pltpu.VMEM((h, s, LANES), BF16),
                        pltpu.VMEM((GROUP_WIDTH, s), BF16)],
        compiler_params=pltpu.CompilerParams(dimension_semantics=("parallel",),
                                             vmem_limit_bytes=VMEM_LIMIT),
        name="mla",
    )(zc, posc, *args)


def _dot_exact(a, b, parts):
    f32_is_lhs = a.dtype == F32
    r = a if f32_is_lhs else b
    terms = []
    for _ in range(parts):
        p = r.astype(BF16)
        terms.append(jnp.dot(p, b, preferred_element_type=F32) if f32_is_lhs
                     else jnp.dot(a, p, preferred_element_type=F32))
        r = r - p.astype(F32)
    acc = terms[-1]
    for t in terms[-2::-1]:
        acc = acc + t
    return acc


def _gdn_kernel(z_ref, cw_ref, alog_ref, dtb_ref, ng_ref, o_ref,
                u_s, wq_s, kd_s, attn_s, gl_s, state_ref):
    s = z_ref.shape[0]
    gw = GROUP_WIDTH
    g_rows = GDN_ROWS
    per = g_rows // CHUNK
    ri = lax.broadcasted_iota(jnp.int32, (g_rows, g_rows), 0)
    ci = lax.broadcasted_iota(jnp.int32, (g_rows, g_rows), 1)
    block_mask = lambda width: (ri // width) == (ci // width)
    eye = ri == ci
    same_block = block_mask(CHUNK)
    causal = same_block & (ri >= ci)
    strict = same_block & (ri > ci)
    tri = causal.astype(BF16)
    ones_bd = same_block.astype(BF16)
    er = lax.broadcasted_iota(jnp.int32, (LANES, gw), 0)
    ec = lax.broadcasted_iota(jnp.int32, (LANES, gw), 1) // HEAD_DIM
    expand_beta = (er == ec).astype(BF16)
    expand_alpha = (er == ec + HEADS).astype(BF16)

    def precompute(gi, carry):
        rows = pl.ds(pl.multiple_of(gi * g_rows, g_rows), g_rows)
        qkv = _silu(_causal_conv_rows(z_ref, gi, g_rows, 0, 3 * gw, cw_ref[...]))
        q = qkv[:, 0:gw]
        k = qkv[:, gw:2 * gw]
        v = qkv[:, 2 * gw:3 * gw]
        qn = q * lax.rsqrt(_dot_exact(q * q, ones_bd, 2) + EPS) * (HEAD_DIM ** -0.5)
        kn = k * lax.rsqrt(_dot_exact(k * k, ones_bd, 2) + EPS)
        small = z_ref[rows, 4 * gw:4 * gw + LANES]
        beta_t = jax.nn.sigmoid(small)
        g_t = -jnp.exp(alog_ref[...]) * _softplus(small + dtb_ref[...])
        beta_w = _dot_exact(beta_t, expand_beta, 3)
        g_w = _dot_exact(g_t, expand_alpha, 3)
        gc_w = _dot_exact(tri, g_w, 3)
        gl_w = _dot_exact(ones_bd, g_w, 3)
        gc_t = _dot_exact(tri, g_t, 3)
        gc_tt = gc_t.T
        egc_w = jnp.exp(gc_w)
        kb_w = kn * beta_w
        vb_w = v * beta_w
        kbe_w = kb_w * egc_w
        heads = [slice(h * HEAD_DIM, (h + 1) * HEAD_DIM) for h in range(HEADS)]
        kh = [kn[:, sl].astype(BF16) for sl in heads]
        grams = [_dot_nt(kb_w[:, sl], kh[h]) for h, sl in enumerate(heads)]
        qks = [_dot_nt(qn[:, sl], kh[h]) for h, sl in enumerate(heads)]
        lowers, rhs = [], []
        for h, sl in enumerate(heads):
            gcol = jnp.broadcast_to(gc_t[:, HEADS + h:HEADS + h + 1], (g_rows, g_rows))
            grow = jnp.broadcast_to(gc_tt[HEADS + h:HEADS + h + 1, :], (g_rows, g_rows))
            decay = jnp.exp(jnp.where(causal, gcol - grow, -jnp.inf))
            lowers.append(jnp.where(strict, grams[h] * decay, 0.0))
            attn_s[h, rows, :] = jnp.where(causal, qks[h] * decay, 0.0).astype(BF16)
            rhs.append(jnp.concatenate([vb_w[:, sl], kbe_w[:, sl]], axis=1))
        bf = lambda xs: [x.astype(BF16) for x in xs]
        plus_eye = lambda x: jnp.where(eye, x + 1.0, x)
        inner = block_mask(SOLVE_BASE)
        d = [jnp.where(inner, low, 0.0) for low in lowers]
        factors = [[jnp.where(eye, 1.0, -x) for x in d]]
        power = bf(d)
        width = 2
        while width < SOLVE_BASE:
            power_f = [_dot(p, p) for p in power]
            factors.append([plus_eye(x) for x in power_f])
            power = bf(power_f)
            width *= 2
        while len(factors) > 1:
            factors = [[_dot(x, y) for x, y in zip(factors[i], factors[i + 1])] if i + 1 < len(factors)
                       else factors[i] for i in range(0, len(factors), 2)]
        t = factors[0]
        width = SOLVE_BASE
        while width < CHUNK:
            outer = block_mask(2 * width)
            e = [jnp.where(outer & ~inner, low, 0.0).astype(BF16) for low in lowers]
            t_b = bf(t)
            et = [_dot(x, y) for x, y in zip(e, t_b)]
            t = [x - _dot(y, z) for x, y, z in zip(t, t_b, et)]
            inner = outer
            width *= 2
        sols = [_dot(x, y) for x, y in zip(t, rhs)]
        u_s[rows, :] = jnp.concatenate([sol[:, :HEAD_DIM] for sol in sols], axis=1)
        w_all = jnp.concatenate([sol[:, HEAD_DIM:] for sol in sols], axis=1).astype(BF16)
        qd_w = (qn * egc_w).astype(BF16)
        kd_s[rows, :] = (kn * jnp.exp(gl_w - gc_w)).astype(BF16)
        for c in range(per):
            blk = slice(c * CHUNK, (c + 1) * CHUNK)
            wq_s[gi * per + c, 0:CHUNK, :] = w_all[blk]
            wq_s[gi * per + c, CHUNK:2 * CHUNK, :] = qd_w[blk]
            gl_s[gi * per + c] = gl_w[c * CHUNK:c * CHUNK + 1, :]
        return carry

    lax.fori_loop(0, s // g_rows, precompute, 0)

    state_ref[...] = jnp.zeros_like(state_ref)

    def recur(gi, carry):
        st = state_ref[...]
        outs = []
        for c in range(per):
            n = gi * per + c
            rows = _chunk_rows(n)
            r1 = _dot(wq_s[n], st)
            v_new = (u_s[rows, :] - r1[0:CHUNK]).astype(BF16)
            ds = lax.dot_general(kd_s[rows, :], v_new, (((0,), (0,)), ((), ())),
                                 preferred_element_type=F32)
            parts = []
            for h in range(HEADS):
                sl = slice(h * HEAD_DIM, (h + 1) * HEAD_DIM)
                a = attn_s[h, rows, c * CHUNK:(c + 1) * CHUNK]
                parts.append(jnp.dot(a, v_new[:, sl], preferred_element_type=F32))
            outs.append(r1[CHUNK:2 * CHUNK] + jnp.concatenate(parts, axis=1))
            st = st * jnp.exp(gl_s[n]) + jnp.where(same_block, ds, 0.0)
        state_ref[...] = st
        rows = pl.ds(pl.multiple_of(gi * g_rows, g_rows), g_rows)
        o = jnp.concatenate(outs, axis=0)
        ms = _dot_exact(o * o, ones_bd, 2) * (1.0 / HEAD_DIM)
        gate = z_ref[rows, 3 * gw:4 * gw]
        o_ref[rows, :] = o * lax.rsqrt(ms + EPS) * ng_ref[...] * _silu(gate)
        return carry

    lax.fori_loop(0, s // g_rows, recur, 0)


def _gated_deltanet(zd, conv_w, a_log, dt_bias, norm_g):
    b, s, _ = zd.shape
    lane_vec = lambda p: jnp.zeros((1, LANES), F32).at[0, HEADS:2 * HEADS].set(p)
    ng = jnp.tile(norm_g, HEADS)[None, :]
    args = (conv_w, lane_vec(a_log), lane_vec(dt_bias), ng)
    full = lambda a: pl.BlockSpec(a.shape, lambda i: (0,) * a.ndim)
    return pl.pallas_call(
        _gdn_kernel,
        grid=(b,),
        in_specs=[pl.BlockSpec((None, s, ZD_W), lambda i: (i, 0, 0))] + [full(a) for a in args],
        out_specs=pl.BlockSpec((None, s, GROUP_WIDTH), lambda i: (i, 0, 0)),
        out_shape=jax.ShapeDtypeStruct((b, s, GROUP_WIDTH), F32),
        scratch_shapes=[pltpu.VMEM((s, GROUP_WIDTH), F32),
                        pltpu.VMEM((s // CHUNK, 2 * CHUNK, GROUP_WIDTH), BF16),
                        pltpu.VMEM((s, GROUP_WIDTH), BF16),
                        pltpu.VMEM((HEADS, s, GDN_ROWS), BF16),
                        pltpu.VMEM((s // CHUNK, 1, GROUP_WIDTH), F32),
                        pltpu.VMEM((GROUP_WIDTH, GROUP_WIDTH), F32)],
        compiler_params=pltpu.CompilerParams(dimension_semantics=("parallel",),
                                             vmem_limit_bytes=VMEM_LIMIT),
        name="gated_deltanet",
    )(zd, *args)


def _mix_out_kernel(x_ref, ya_ref, yb_ref, yc_ref, yd_ref, wo_ref, g_ref, wq_ref, x1_ref, q_ref):
    acc = x_ref[...]
    for gi, y_ref in enumerate((ya_ref, yb_ref, yc_ref, yd_ref)):
        acc = acc + jnp.dot(y_ref[...].astype(BF16), wo_ref[gi * GROUP_WIDTH:(gi + 1) * GROUP_WIDTH, :],
                            preferred_element_type=F32)
    x1_ref[...] = acc
    hq = (_rms(acc, D_MODEL) * g_ref[...]).astype(BF16)
    q_ref[...] = jnp.dot(hq, wq_ref[...], preferred_element_type=F32).astype(BF16)


def _mix_out(x2d, ys, w_out, gain, wq):
    t = x2d.shape[0]
    row = lambda w: pl.BlockSpec((ROW_TILE, w), lambda i: (i, 0))
    full = lambda a: pl.BlockSpec(a.shape, lambda i: (0,) * a.ndim)
    return pl.pallas_call(
        _mix_out_kernel,
        grid=(t // ROW_TILE,),
        in_specs=[row(D_MODEL)] + [row(GROUP_WIDTH)] * 4 + [full(w_out), full(gain), full(wq)],
        out_specs=[row(D_MODEL), row(D_MODEL)],
        out_shape=[jax.ShapeDtypeStruct((t, D_MODEL), F32), jax.ShapeDtypeStruct((t, D_MODEL), BF16)],
        compiler_params=pltpu.CompilerParams(dimension_semantics=("parallel",),
                                             vmem_limit_bytes=VMEM_LIMIT),
        name="mix_out",
    )(x2d, *ys, w_out, gain, wq)


def _mem_kv_kernel(mem_ref, g_ref, wkv_ref, k_ref, v_ref):
    m = (_rms(mem_ref[...], D_MODEL) * g_ref[...]).astype(BF16)
    k_ref[...] = jnp.dot(m, wkv_ref[:, :D_MODEL], preferred_element_type=F32).astype(BF16)
    v_ref[...] = jnp.dot(m, wkv_ref[:, D_MODEL:], preferred_element_type=F32).astype(BF16)


def _mem_kv(mem, gain, wkv):
    b, m, _ = mem.shape
    full = lambda a: pl.BlockSpec(a.shape, lambda i: (0,) * a.ndim)
    blk = pl.BlockSpec((None, m, D_MODEL), lambda i: (i, 0, 0))
    return pl.pallas_call(
        _mem_kv_kernel,
        grid=(b,),
        in_specs=[blk, full(gain), full(wkv)],
        out_specs=[blk, blk],
        out_shape=[jax.ShapeDtypeStruct((b, m, D_MODEL), BF16)] * 2,
        compiler_params=pltpu.CompilerParams(dimension_semantics=("parallel",),
                                             vmem_limit_bytes=VMEM_LIMIT),
        name="mem_kv",
    )(mem, gain, wkv)


def _xattn_mlp_kernel(x_ref, q_ref, k_ref, v_ref, wo_ref, g_ref, w1_ref, w2_ref, fg_ref, o_ref,
                      *, final_norm):
    x = x_ref[...]
    q = q_ref[...]
    heads = []
    for h in range(XA_HEADS):
        sl = slice(h * XA_HEAD_DIM, (h + 1) * XA_HEAD_DIM)
        sc = lax.dot_general(q[:, sl], k_ref[:, sl], (((1,), (1,)), ((), ())),
                             preferred_element_type=F32) * (XA_HEAD_DIM ** -0.5)
        e = jnp.exp(sc - jnp.max(sc, axis=-1, keepdims=True))
        p = e / jnp.sum(e, axis=-1, keepdims=True)
        heads.append(jnp.dot(p.astype(BF16), v_ref[:, sl], preferred_element_type=F32).astype(BF16))
    o = jnp.concatenate(heads, axis=1)
    x = x + jnp.dot(o, wo_ref[...], preferred_element_type=F32)
    hm = (_rms(x, D_MODEL) * g_ref[...]).astype(BF16)
    for c in range(D_FF // FF_SLAB):
        sl = slice(c * FF_SLAB, (c + 1) * FF_SLAB)
        a = jnp.maximum(jnp.dot(hm, w1_ref[:, sl], preferred_element_type=F32), 0.0)
        x = x + jnp.dot((a * a).astype(BF16), w2_ref[sl, :], preferred_element_type=F32)
    if final_norm:
        x = _rms(x, D_MODEL) * fg_ref[...]
    o_ref[...] = x


def _xattn_mlp(x1, q, k, v, wo, gain, w1, w2, final_gain, final_norm):
    b, s, _ = x1.shape
    m = k.shape[1]
    row = pl.BlockSpec((None, ROW_TILE, D_MODEL), lambda i, j: (i, j, 0))
    mem = pl.BlockSpec((None, m, D_MODEL), lambda i, j: (i, 0, 0))
    const = lambda a: pl.BlockSpec(a.shape, lambda i, j: (0,) * a.ndim, pipeline_mode=pl.Buffered(1))
    return pl.pallas_call(
        functools.partial(_xattn_mlp_kernel, final_norm=final_norm),
        grid=(b, s // ROW_TILE),
        in_specs=[row, row, mem, mem, const(wo), const(gain), const(w1), const(w2), const(final_gain)],
        out_specs=row,
        out_shape=jax.ShapeDtypeStruct((b, s, D_MODEL), F32),
        compiler_params=pltpu.CompilerParams(dimension_semantics=("parallel", "parallel"),
                                             vmem_limit_bytes=VMEM_LIMIT),
        name="xattn_mlp",
    )(x1, q, k, v, wo, gain, w1, w2, final_gain)


def _pack_w_in(w):
    d = w.shape[0]
    zeros = lambda n: jnp.zeros((d, n), w.dtype)
    c0 = ZA_W + ZB_W
    cq = w[:, c0:c0 + MLA_Q_LORA]
    ckv = w[:, c0 + MLA_Q_LORA:c0 + MLA_Q_LORA + MLA_KV_LORA]
    kr = w[:, c0 + MLA_Q_LORA + MLA_KV_LORA:c0 + MLA_Q_LORA + MLA_KV_LORA + MLA_ROPE]
    d0 = c0 + MLA_Q_LORA + MLA_KV_LORA + MLA_ROPE
    rest = w[:, d0:]
    packed = jnp.concatenate([
        w[:, :c0],
        cq, zeros(ZC_KV - MLA_Q_LORA), ckv, zeros(MLA_NOPE), kr, zeros(LANES - MLA_NOPE - MLA_ROPE),
        rest, zeros(ZD_W - rest.shape[1]),
    ], axis=1)
    assert packed.shape[1] == Z_W
    return packed.astype(BF16)


def kernel(x, mem, positions, norm_mix, w_in, ret_gn, lru_conv_w, lru_conv_b, lru_wa, lru_ba, lru_wx, lru_bx, lru_lambda, mla_q_norm, mla_w_uq, mla_kv_norm, mla_w_ukv, gdn_conv_w, gdn_a_log, gdn_dt_bias, gdn_norm, w_out, norm_xattn, norm_mem, xa_wq, xa_wkv, xa_wo, norm_mlp, mlp_w1, mlp_w2, final_norm):
    b, s, d = x.shape
    depth = w_in.shape[0]
    pos_f = positions.astype(F32)
    vec = lambda v: v.reshape(1, -1)
    for l in range(depth):
        za, zb, zc, zd = _in_proj(x.reshape(b * s, d), vec(norm_mix[l]), _pack_w_in(w_in[l]))
        seq = lambda z: z.reshape(b, s, z.shape[-1])
        y_a = _retention(seq(za), pos_f, ret_gn[l])
        y_b = _rg_lru(seq(zb), lru_conv_w[l], lru_conv_b[l], lru_wa[l], lru_ba[l], lru_wx[l],
                      lru_bx[l], lru_lambda[l])
        y_c = _mla(seq(zc), pos_f, mla_q_norm[l], mla_w_uq[l], mla_kv_norm[l], mla_w_ukv[l])
        y_d = _gated_deltanet(seq(zd), gdn_conv_w[l], gdn_a_log[l], gdn_dt_bias[l], gdn_norm[l])
        ys = [y.reshape(b * s, GROUP_WIDTH) for y in (y_a, y_b, y_c, y_d)]
        x1, q = _mix_out(x.reshape(b * s, d), ys, w_out[l].astype(BF16), vec(norm_xattn[l]),
                         xa_wq[l].astype(BF16))
        k, v = _mem_kv(mem, vec(norm_mem[l]), xa_wkv[l].astype(BF16))
        x = _xattn_mlp(x1.reshape(b, s, d), q.reshape(b, s, d), k, v, xa_wo[l].astype(BF16),
                       vec(norm_mlp[l]), mlp_w1[l].astype(BF16), mlp_w2[l].astype(BF16),
                       vec(final_norm), l == depth - 1)
    return x
```

```python
import functools

import jax
import jax.numpy as jnp
from jax import lax
from jax.experimental import pallas as pl
from jax.experimental.pallas import tpu as pltpu

F32 = jnp.float32
BF16 = jnp.bfloat16

D_MODEL = 1024
CHUNK = 64
HEADS = 4
HEAD_DIM = 64
GROUP_WIDTH = HEADS * HEAD_DIM
EPS = 1e-6
ROPE_THETA = 10000.0
LRU_C = 8.0
MLA_Q_LORA = 192
MLA_KV_LORA = 128
MLA_ROPE = 32
MLA_NOPE = 64
XA_HEADS = 4
XA_HEAD_DIM = D_MODEL // XA_HEADS
D_FF = 4 * D_MODEL
FF_SLAB = 1024
ATT_TILE = 512
ROPE_TILE = 256
GDN_ROWS = 256
SOLVE_BASE = 16
assert GDN_ROWS == GROUP_WIDTH and HEAD_DIM == CHUNK
ROW_TILE = 512
SUBLANES = 8
LANES = 128
VMEM_LIMIT = 56 * 1024 * 1024

ZA_W = 4 * GROUP_WIDTH
ZB_W = 2 * GROUP_WIDTH
ZC_W = 512
ZD_W = 1152
ZC_KV = 256
ZC_KR = 384
Z_W = ZA_W + ZB_W + ZC_W + ZD_W


def _dot(a, b):
    return jnp.dot(a.astype(BF16), b.astype(BF16), preferred_element_type=F32)


def _dot_nt(a, b):
    return lax.dot_general(a.astype(BF16), b.astype(BF16), (((1,), (1,)), ((), ())),
                           preferred_element_type=F32)


def _dot_tn(a, b):
    return lax.dot_general(a.astype(BF16), b.astype(BF16), (((0,), (0,)), ((), ())),
                           preferred_element_type=F32)


def _split3(x):
    x1 = x.astype(BF16)
    r1 = x - x1.astype(F32)
    x2 = r1.astype(BF16)
    x3 = (r1 - x2.astype(F32)).astype(BF16)
    return x1, x2, x3


def _dot3(a, b):
    a1 = a.astype(BF16)
    a2 = (a - a1.astype(F32)).astype(BF16)
    b1 = b.astype(BF16)
    b2 = (b - b1.astype(F32)).astype(BF16)
    d = functools.partial(jnp.dot, preferred_element_type=F32)
    return d(a1, b1) + (d(a1, b2) + d(a2, b1))


def _silu(x):
    return x * jax.nn.sigmoid(x)


def _softplus(x):
    return jnp.maximum(x, 0.0) + jnp.log1p(jnp.exp(-jnp.abs(x)))


def _expm1(y):
    e = jnp.exp(y)
    near = (y > -0.5) & (e != 1.0)
    safe_log = jnp.where(near, jnp.log(e), 1.0)
    return jnp.where(near, (e - 1.0) * y / safe_log, jnp.where(e == 1.0, y, e - 1.0))


def _rms(x, n):
    return x * lax.rsqrt(jnp.sum(x * x, axis=-1, keepdims=True) * (1.0 / n) + EPS)


def _chunk_rows(n):
    return pl.ds(pl.multiple_of(n * CHUNK, CHUNK), CHUNK)


def _causal_conv_rows(z_ref, n, nrows, lo, hi, w):
    cur = z_ref[pl.ds(pl.multiple_of(n * nrows, nrows), nrows), lo:hi]
    prev_start = pl.multiple_of(jnp.maximum(n * nrows - SUBLANES, 0), SUBLANES)
    prev = z_ref[pl.ds(prev_start, SUBLANES), lo:hi]
    prev = jnp.where(n > 0, prev, 0.0)
    tile = jnp.concatenate([prev, cur], axis=0)
    acc = cur * w[3:4, :]
    for k in (1, 2, 3):
        acc = acc + pltpu.roll(tile, k, 0)[SUBLANES:, :] * w[3 - k:4 - k, :]
    return acc


def _lane_block_swap(x, half, first):
    return jnp.where(first, pltpu.roll(x, LANES - half, 1), pltpu.roll(x, half, 1))


def _in_proj_kernel(x_ref, g_ref, w_ref, za_ref, zb_ref, zc_ref, zd_ref):
    x = x_ref[...]
    h = (_rms(x, D_MODEL) * g_ref[...]).astype(BF16)
    lo = 0
    for ref in (za_ref, zb_ref, zc_ref, zd_ref):
        hi = lo + ref.shape[-1]
        ref[...] = jnp.dot(h, w_ref[:, lo:hi], preferred_element_type=F32)
        lo = hi


def _in_proj(x2d, gain, w_packed):
    t = x2d.shape[0]
    row = lambda w: pl.BlockSpec((ROW_TILE, w), lambda i: (i, 0))
    full = lambda a: pl.BlockSpec(a.shape, lambda i: (0,) * a.ndim)
    return pl.pallas_call(
        _in_proj_kernel,
        grid=(t // ROW_TILE,),
        in_specs=[row(D_MODEL), full(gain), full(w_packed)],
        out_specs=[row(ZA_W), row(ZB_W), row(ZC_W), row(ZD_W)],
        out_shape=[jax.ShapeDtypeStruct((t, w), F32) for w in (ZA_W, ZB_W, ZC_W, ZD_W)],
        compiler_params=pltpu.CompilerParams(dimension_semantics=("parallel",),
                                             vmem_limit_bytes=VMEM_LIMIT),
        name="in_proj",
    )(x2d, gain, w_packed)


def _compact_positions(pos_f, n_freq):
    b, s = pos_f.shape
    nb = LANES // n_freq
    r = ROPE_TILE // nb
    p = pos_f.reshape(b, s // ROPE_TILE, nb, r).transpose(0, 1, 3, 2)
    return jnp.repeat(p, n_freq, axis=-1)


def _rope_expanders(n_freq, width, rope_lo, rope_hi):
    nb = LANES // n_freq
    src = jnp.arange(LANES)
    lane = jnp.arange(width)
    on_rope = (lane >= rope_lo) & (lane < rope_hi)
    hit = (src[:, None] % n_freq == lane[None, :] % n_freq) & on_rope[None, :]
    return jnp.stack([(hit & (src[:, None] // n_freq == r)) for r in range(nb)]).astype(BF16)


def _expand_rope_table(t_c, e_ref):
    return jnp.concatenate([_dot_exact(t_c, e_ref[r], 3) for r in range(e_ref.shape[0])], axis=0)


def _retention_kernel(z_ref, posc_ref, invf_ref, e_ref, dmat_ref, xi_ref, zeta_ref, cd_ref, gn_ref,
                      o_ref, state_ref):
    s = z_ref.shape[0]
    gw = GROUP_WIDTH
    n_rows = ROPE_TILE
    lane = lax.broadcasted_iota(jnp.int32, (n_rows, LANES), 1)
    first = (lane % HEAD_DIM) < (HEAD_DIM // 2)
    lane_wide = lax.broadcasted_iota(jnp.int32, (n_rows, gw), 1)
    first_wide = (lane_wide % HEAD_DIM) < (HEAD_DIM // 2)
    ri = lax.broadcasted_iota(jnp.int32, (gw, gw), 0)
    ci = lax.broadcasted_iota(jnp.int32, (gw, gw), 1)
    same_head = (ri // HEAD_DIM) == (ci // HEAD_DIM)
    ones_bd = same_head.astype(BF16)
    state_ref[...] = jnp.zeros_like(state_ref)

    def rope(x, cos, sin_signed):
        parts = []
        for c in range(gw // LANES):
            sl = slice(c * LANES, (c + 1) * LANES)
            xs = _lane_block_swap(x[:, sl], HEAD_DIM // 2, first)
            parts.append(x[:, sl] * cos[:, sl] + xs * sin_signed[:, sl])
        return jnp.concatenate(parts, axis=1)

    def body(g, carry):
        rows = pl.ds(pl.multiple_of(g * n_rows, n_rows), n_rows)
        ang = posc_ref[g] * invf_ref[...]
        cos = _expand_rope_table(jnp.cos(ang), e_ref)
        sin = _expand_rope_table(jnp.sin(ang), e_ref)
        sin_signed = jnp.where(first_wide, -sin, sin)
        q = (rope(z_ref[rows, 0:gw], cos, sin_signed) * (HEAD_DIM ** -0.5)).astype(BF16)
        k = rope(z_ref[rows, gw:2 * gw], cos, sin_signed)
        v = z_ref[rows, 2 * gw:3 * gw].astype(BF16)
        kz = (k * zeta_ref[...]).astype(BF16)
        k = k.astype(BF16)
        heads = [slice(h * HEAD_DIM, (h + 1) * HEAD_DIM) for h in range(HEADS)]
        blocks = [slice(c * CHUNK, (c + 1) * CHUNK) for c in range(n_rows // CHUNK)]
        scores = [_dot_nt(q[:, sl], k[:, sl]) for sl in heads]
        kvs = [_dot_tn(kz[blk], v[blk]) for blk in blocks]
        intra = [_dot(scores[h] * dmat_ref[h], v[:, sl])
                 for h, sl in enumerate(heads)]
        st = state_ref[...]
        cross = []
        for c, blk in enumerate(blocks):
            cross.append(_dot(q[blk], st))
            st = st * cd_ref[...] + jnp.where(same_head, kvs[c], 0.0)
        state_ref[...] = st
        o = jnp.concatenate(intra, axis=1) + jnp.concatenate(cross, axis=0) * xi_ref[...]
        mu = _dot_exact(o, ones_bd, 2) * (1.0 / HEAD_DIM)
        d = o - mu
        var = _dot_exact(d * d, ones_bd, 2) * (1.0 / HEAD_DIM)
        gate = z_ref[rows, 3 * gw:4 * gw]
        o_ref[rows, :] = d * lax.rsqrt(var + EPS) * gn_ref[...] * _silu(gate)
        return carry

    lax.fori_loop(0, s // n_rows, body, 0)


def _retention(za, pos_f, ret_gn):
    b, s, _ = za.shape
    h = HEADS
    half = HEAD_DIM // 2
    per = ROPE_TILE // CHUNK
    inv_freq = jnp.power(ROPE_THETA, -jnp.arange(half, dtype=F32) / half)
    invf = jnp.tile(inv_freq, LANES // half)[None, :]
    posc = _compact_positions(pos_f, half)
    expand = _rope_expanders(half, GROUP_WIDTH, 0, GROUP_WIDTH)
    log_gamma = jnp.log1p(-jnp.exp2(-5.0 - jnp.arange(h, dtype=F32)))
    idx = jnp.arange(CHUNK, dtype=F32)
    dmat = jnp.exp(jnp.abs(idx[:, None] - idx[None, :])[None] * log_gamma[:, None, None])
    dmat = jnp.stack([jnp.kron(jnp.eye(per, dtype=F32), dmat[i]) for i in range(h)])
    xi = jnp.repeat(jnp.exp((idx[:, None] + 1.0) * log_gamma), HEAD_DIM, axis=1)
    zeta = jnp.repeat(jnp.exp((CHUNK - 1.0 - idx)[:, None] * log_gamma), HEAD_DIM, axis=1)
    xi = jnp.tile(xi, (per, 1))
    zeta = jnp.tile(zeta, (per, 1))
    cd = jnp.repeat(jnp.exp(CHUNK * log_gamma), HEAD_DIM)[None, :]
    gn = ret_gn.reshape(1, GROUP_WIDTH)
    consts = (invf, expand, dmat, xi, zeta, cd, gn)
    full = lambda a: pl.BlockSpec(a.shape, lambda i: (0,) * a.ndim)
    return pl.pallas_call(
        _retention_kernel,
        grid=(b,),
        in_specs=[pl.BlockSpec((None, s, ZA_W), lambda i: (i, 0, 0)),
                  pl.BlockSpec((None,) + posc.shape[1:], lambda i: (i, 0, 0, 0))]
                 + [full(a) for a in consts],
        out_specs=pl.BlockSpec((None, s, GROUP_WIDTH), lambda i: (i, 0, 0)),
        out_shape=jax.ShapeDtypeStruct((b, s, GROUP_WIDTH), F32),
        scratch_shapes=[pltpu.VMEM((GROUP_WIDTH, GROUP_WIDTH), F32)],
        compiler_params=pltpu.CompilerParams(dimension_semantics=("parallel",),
                                             vmem_limit_bytes=VMEM_LIMIT),
        name="retention",
    )(za, posc, *consts)


def _interleave(*stage_generators):
    live = list(stage_generators)
    while live:
        for gen in list(live):
            try:
                next(gen)
            except StopIteration:
                live.remove(gen)


def _lru_stages(gi, z_ref, cw_ref, cb_ref, wa_ref, ba_ref, wx_ref, bx_ref, lam_ref, o_ref, h_ref):
    w = GROUP_WIDTH
    row = lax.broadcasted_iota(jnp.int32, (CHUNK, w), 0)
    for c in range(GDN_ROWS // CHUNK):
        n = gi * (GDN_ROWS // CHUNK) + c
        rows = _chunk_rows(n)
        xc = _causal_conv_rows(z_ref, n, CHUNK, 0, w, cw_ref[...]) + cb_ref[...]
        r = jax.nn.sigmoid(_dot(xc, wa_ref[...]) + ba_ref[...])
        i = jax.nn.sigmoid(_dot(xc, wx_ref[...]) + bx_ref[...])
        log_a = -LRU_C * r * _softplus(-lam_ref[...])
        a = jnp.exp(log_a)
        u = jnp.sqrt(-_expm1(2.0 * log_a)) * (i * xc)
        d = 1
        while d < CHUNK:
            a_sh = jnp.where(row >= d, pltpu.roll(a, d, 0), 1.0)
            u_sh = jnp.where(row >= d, pltpu.roll(u, d, 0), 0.0)
            u = a * u_sh + u
            a = a * a_sh
            d *= 2
        hs = u + a * h_ref[...]
        h_ref[...] = hs[CHUNK - 1:CHUNK, :]
        gate = z_ref[rows, w:2 * w]
        o_ref[rows, :] = hs * jax.nn.gelu(gate, approximate=True)
        yield


def _mla_kernel(z_ref, posc_ref, invf_ref, e_ref, base_ref, qn_ref, wq_ref, kvn_ref, wk_ref, wvt_ref,
                o_ref, q_s, k_s, vt_s):
    s = z_ref.shape[0]
    nt = s // ATT_TILE
    scale = (MLA_NOPE + MLA_ROPE) ** -0.5
    lane = lax.broadcasted_iota(jnp.int32, (ROPE_TILE, LANES), 1)
    first = (lane >= MLA_NOPE) & (lane < MLA_NOPE + MLA_ROPE // 2)

    def rope(x, cos, sin_signed):
        return x * cos + _lane_block_swap(x, MLA_ROPE // 2, first) * sin_signed

    def project(t, carry):
        rows = pl.ds(pl.multiple_of(t * ROPE_TILE, ROPE_TILE), ROPE_TILE)
        ang = posc_ref[t] * invf_ref[...]
        cos = _expand_rope_table(jnp.cos(ang), e_ref) + base_ref[...]
        sin = _expand_rope_table(jnp.sin(ang), e_ref)
        sin_signed = jnp.where(first, -sin, sin)
        cq = (_rms(z_ref[rows, 0:ZC_KV], MLA_Q_LORA) * qn_ref[...]).astype(BF16)
        ckv_f = _rms(z_ref[rows, ZC_KV:ZC_KR], MLA_KV_LORA) * kvn_ref[...]
        ckv = ckv_f.astype(BF16)
        k_rope = rope(z_ref[rows, ZC_KR:ZC_W], cos, sin_signed)
        vt_s[:, rows] = jnp.dot(wvt_ref[...], ckv_f.T.astype(BF16),
                                preferred_element_type=F32).astype(BF16)
        for h in range(HEADS):
            q = jnp.dot(cq, wq_ref[h], preferred_element_type=F32)
            q_s[h, rows, :] = rope(q, cos, sin_signed).astype(BF16)
            k = jnp.dot(ckv, wk_ref[h], preferred_element_type=F32)
            k_s[h, rows, :] = (k + k_rope).astype(BF16)
        return carry

    lax.fori_loop(0, s // ROPE_TILE, project, 0)

    key_id = lax.broadcasted_iota(jnp.int32, (ATT_TILE, ATT_TILE), 0) // CHUNK
    qry_id = lax.broadcasted_iota(jnp.int32, (ATT_TILE, ATT_TILE), 1) // CHUNK
    diag_visible = key_id <= qry_id

    def q_tile(i, carry):
        rows = pl.ds(pl.multiple_of(i * ATT_TILE, ATT_TILE), ATT_TILE)

        def kv_tile(j, state, on_diagonal):
            cols = pl.ds(pl.multiple_of(j * ATT_TILE, ATT_TILE), ATT_TILE)
            scores = [lax.dot_general(k_s[h, cols, :], q_s[h, rows, :], (((1,), (1,)), ((), ())),
                                      preferred_element_type=F32) for h in range(HEADS)]
            new_state = []
            for h in range(HEADS):
                m, l, acc = state[h]
                sc = scores[h] * scale
                if on_diagonal:
                    sc = jnp.where(diag_visible, sc, -jnp.inf)
                m_new = jnp.maximum(m, jnp.max(sc, axis=0, keepdims=True))
                p = jnp.exp(sc - m_new)
                alpha = jnp.exp(m - m_new)
                l = alpha * l + jnp.sum(p, axis=0, keepdims=True)
                vt = vt_s[h * HEAD_DIM:(h + 1) * HEAD_DIM, cols]
                acc = alpha * acc + jnp.dot(vt, p.astype(BF16), preferred_element_type=F32)
                new_state.append((m_new, l, acc))
            return tuple(new_state)

        init = tuple((jnp.full((1, ATT_TILE), -jnp.inf, F32), jnp.zeros((1, ATT_TILE), F32),
                      jnp.zeros((HEAD_DIM, ATT_TILE), F32)) for _ in range(HEADS))
        state = lax.fori_loop(0, i, lambda j, st: kv_tile(j, st, False), init)
        state = kv_tile(i, state, True)
        o_t = jnp.concatenate([acc / l for _, l, acc in state], axis=0)
        o_ref[rows, :] = o_t.T
        return carry

    lax.fori_loop(0, nt, q_tile, 0)


def _mla(zc, pos_f, q_norm, w_uq, kv_norm, w_ukv):
    b, s, _ = zc.shape
    h = HEADS
    half = MLA_ROPE // 2
    inv_freq = jnp.power(ROPE_THETA, -jnp.arange(half, dtype=F32) / half)
    invf = jnp.tile(inv_freq, LANES // half)[None, :]
    posc = _compact_positions(pos_f, half)
    expand = _rope_expanders(half, LANES, MLA_NOPE, MLA_NOPE + MLA_ROPE)
    lane = jnp.arange(LANES)
    base = ((lane < MLA_NOPE) | (lane >= MLA_NOPE + MLA_ROPE)).astype(F32)[None, :]
    qn = jnp.zeros((1, ZC_KV), F32).at[0, :MLA_Q_LORA].set(q_norm)
    kvn = kv_norm.reshape(1, MLA_KV_LORA)
    wq = w_uq.reshape(MLA_Q_LORA, h, MLA_NOPE + MLA_ROPE).transpose(1, 0, 2)
    wq = jnp.pad(wq, ((0, 0), (0, ZC_KV - MLA_Q_LORA), (0, LANES - MLA_NOPE - MLA_ROPE))).astype(BF16)
    wkv = w_ukv.reshape(MLA_KV_LORA, h, MLA_NOPE + HEAD_DIM)
    wk = jnp.pad(wkv[:, :, :MLA_NOPE].transpose(1, 0, 2),
                 ((0, 0), (0, 0), (0, LANES - MLA_NOPE))).astype(BF16)
    wvt = wkv[:, :, MLA_NOPE:].reshape(MLA_KV_LORA, h * HEAD_DIM).T.astype(BF16)
    args = (invf, expand, base, qn, wq, kvn, wk, wvt)
    full = lambda a: pl.BlockSpec(a.shape, lambda i: (0,) * a.ndim)
    return pl.pallas_call(
        _mla_kernel,
        grid=(b,),
        in_specs=[pl.BlockSpec((None, s, ZC_W), lambda i: (i, 0, 0)),
                  pl.BlockSpec((None,) + posc.shape[1:], lambda i: (i, 0, 0, 0))]
                 + [full(a) for a in args],
        out_specs=pl.BlockSpec((None, s, GROUP_WIDTH), lambda i: (i, 0, 0)),
        out_shape=jax.ShapeDtypeStruct((b, s, GROUP_WIDTH), F32),
        scratch_shapes=[pltpu.VMEM((h, s, LANES), BF16), pltpu.VMEM((h, s, LANES), BF16),
                        pltpu.VMEM((GROUP_WIDTH, s), BF16)],
        compiler_params=pltpu.CompilerParams(dimension_semantics=("parallel",),
                                             vmem_limit_bytes=VMEM_LIMIT),
        name="mla",
    )(zc, posc, *args)


def _bf16_parts(x, parts):
    out = []
    for _ in range(parts):
        p = x.astype(BF16)
        out.append(p)
        x = x - p.astype(F32)
    return out


def _dot_exact(a, b, parts=None):
    f32_is_lhs = isinstance(a, list) or a.dtype == F32
    split = a if f32_is_lhs else b
    if not isinstance(split, list):
        split = _bf16_parts(split, parts)
    terms = [jnp.dot(p, b, preferred_element_type=F32) if f32_is_lhs
             else jnp.dot(a, p, preferred_element_type=F32) for p in split]
    acc = terms[-1]
    for t in terms[-2::-1]:
        acc = acc + t
    return acc


def _gdn_lru_kernel(z_ref, cw_ref, alog_ref, dtb_ref, ng_ref,
                    zb_ref, lcw_ref, lcb_ref, lwa_ref, lba_ref, lwx_ref, lbx_ref, llam_ref,
                    o_ref, ob_ref, u_s, wq_s, kd_s, attn_s, gl_s, state_ref, h_ref):
    s = z_ref.shape[0]
    gw = GROUP_WIDTH
    g_rows = GDN_ROWS
    per = g_rows // CHUNK
    ri = lax.broadcasted_iota(jnp.int32, (g_rows, g_rows), 0)
    ci = lax.broadcasted_iota(jnp.int32, (g_rows, g_rows), 1)
    block_mask = lambda width: (ri // width) == (ci // width)
    eye = ri == ci
    same_block = block_mask(CHUNK)
    causal = same_block & (ri >= ci)
    strict = same_block & (ri > ci)
    tri = causal.astype(BF16)
    ones_bd = same_block.astype(BF16)
    er = lax.broadcasted_iota(jnp.int32, (LANES, gw), 0)
    ec = lax.broadcasted_iota(jnp.int32, (LANES, gw), 1) // HEAD_DIM
    expand_beta = (er == ec).astype(BF16)
    expand_alpha = (er == ec + HEADS).astype(BF16)

    def precompute(gi):
        rows = pl.ds(pl.multiple_of(gi * g_rows, g_rows), g_rows)
        qkv = _silu(_causal_conv_rows(z_ref, gi, g_rows, 0, 3 * gw, cw_ref[...]))
        q = qkv[:, 0:gw]
        k = qkv[:, gw:2 * gw]
        v = qkv[:, 2 * gw:3 * gw]
        small = z_ref[rows, 4 * gw:4 * gw + LANES]
        beta_t = jax.nn.sigmoid(small)
        g_t = -jnp.exp(alog_ref[...]) * _softplus(small + dtb_ref[...])
        qn = q * lax.rsqrt(_dot_exact(q * q, ones_bd, 2) + EPS) * (HEAD_DIM ** -0.5)
        kn = k * lax.rsqrt(_dot_exact(k * k, ones_bd, 2) + EPS)
        beta_w = _dot_exact(beta_t, expand_beta, 3)
        g_t_parts = _bf16_parts(g_t, 3)
        g_w = _dot_exact(g_t_parts, expand_alpha)
        gc_t = _dot_exact(tri, g_t_parts)
        yield
        g_w_parts = _bf16_parts(g_w, 3)
        gc_w = _dot_exact(tri, g_w_parts)
        gl_w = _dot_exact(ones_bd, g_w_parts)
        gc_tt = gc_t.T
        kb_w = kn * beta_w
        vb_w = v * beta_w
        heads = [slice(h * HEAD_DIM, (h + 1) * HEAD_DIM) for h in range(HEADS)]
        kh = [kn[:, sl].astype(BF16) for sl in heads]
        grams = [_dot_nt(kb_w[:, sl], kh[h]) for h, sl in enumerate(heads)]
        yield
        qks = [_dot_nt(qn[:, sl], kh[h]) for h, sl in enumerate(heads)]
        egc_w = jnp.exp(gc_w)
        kbe_w = kb_w * egc_w
        yield
        lowers, rhs = [], []
        for h, sl in enumerate(heads):
            gcol = jnp.broadcast_to(gc_t[:, HEADS + h:HEADS + h + 1], (g_rows, g_rows))
            grow = jnp.broadcast_to(gc_tt[HEADS + h:HEADS + h + 1, :], (g_rows, g_rows))
            decay = jnp.exp(jnp.where(causal, gcol - grow, -jnp.inf))
            lowers.append(jnp.where(strict, grams[h] * decay, 0.0))
            attn_s[h, rows, :] = (qks[h] * decay).astype(BF16)
            rhs.append(jnp.concatenate([vb_w[:, sl], kbe_w[:, sl]], axis=1))
        bf = lambda xs: [x.astype(BF16) for x in xs]
        plus_eye = lambda x: jnp.where(eye, x + 1.0, x)
        inner = block_mask(SOLVE_BASE)
        d = [jnp.where(inner, low, 0.0) for low in lowers]
        factors = [[jnp.where(eye, 1.0, -x) for x in d]]
        power = bf(d)
        width = 2
        while width < SOLVE_BASE:
            power_f = [_dot(p, p) for p in power]
            yield
            factors.append([plus_eye(x) for x in power_f])
            power = bf(power_f)
            width *= 2
        while len(factors) > 1:
            factors = [[_dot(x, y) for x, y in zip(factors[i], factors[i + 1])] if i + 1 < len(factors)
                       else factors[i] for i in range(0, len(factors), 2)]
            yield
        t = factors[0]
        width = SOLVE_BASE
        while width < CHUNK:
            outer = block_mask(2 * width)
            e = [jnp.where(outer & ~inner, low, 0.0).astype(BF16) for low in lowers]
            t_b = bf(t)
            et = [_dot(x, y) for x, y in zip(e, t_b)]
            yield
            t = [x - _dot(y, z) for x, y, z in zip(t, t_b, et)]
            yield
            inner = outer
            width *= 2
        sols = [_dot(x, y) for x, y in zip(t, rhs)]
        yield
        u_s[rows, :] = jnp.concatenate([sol[:, :HEAD_DIM] for sol in sols], axis=1)
        w_all = jnp.concatenate([sol[:, HEAD_DIM:] for sol in sols], axis=1).astype(BF16)
        qd_w = (qn * egc_w).astype(BF16)
        kd_s[rows, :] = (kn * jnp.exp(gl_w - gc_w)).astype(BF16)
        for c in range(per):
            blk = slice(c * CHUNK, (c + 1) * CHUNK)
            wq_s[gi * per + c, 0:CHUNK, :] = w_all[blk]
            wq_s[gi * per + c, CHUNK:2 * CHUNK, :] = qd_w[blk]
            gl_s[gi * per + c] = gl_w[c * CHUNK:c * CHUNK + 1, :]

    def recur(gi):
        st = state_ref[...]
        outs = []
        for c in range(per):
            n = gi * per + c
            rows = _chunk_rows(n)
            r1 = _dot(wq_s[n], st)
            yield
            v_new = (u_s[rows, :] - r1[0:CHUNK]).astype(BF16)
            ds = lax.dot_general(kd_s[rows, :], v_new, (((0,), (0,)), ((), ())),
                                 preferred_element_type=F32)
            parts = []
            for h in range(HEADS):
                sl = slice(h * HEAD_DIM, (h + 1) * HEAD_DIM)
                a = attn_s[h, rows, c * CHUNK:(c + 1) * CHUNK]
                parts.append(jnp.dot(a, v_new[:, sl], preferred_element_type=F32))
            yield
            outs.append(r1[CHUNK:2 * CHUNK] + jnp.concatenate(parts, axis=1))
            st = st * jnp.exp(gl_s[n]) + jnp.where(same_block, ds, 0.0)
        state_ref[...] = st
        rows = pl.ds(pl.multiple_of(gi * g_rows, g_rows), g_rows)
        o = jnp.concatenate(outs, axis=0)
        ms = _dot_exact(o * o, ones_bd, 2) * (1.0 / HEAD_DIM)
        gate = z_ref[rows, 3 * gw:4 * gw]
        o_ref[rows, :] = o * lax.rsqrt(ms + EPS) * ng_ref[...] * _silu(gate)

    def lru(gi):
        return _lru_stages(gi, zb_ref, lcw_ref, lcb_ref, lwa_ref, lba_ref, lwx_ref, lbx_ref, llam_ref,
                           ob_ref, h_ref)

    state_ref[...] = jnp.zeros_like(state_ref)
    h_ref[...] = jnp.zeros_like(h_ref)
    n_groups = s // g_rows
    _interleave(precompute(0), lru(0))

    def steady(gi, carry):
        _interleave(recur(gi - 1), precompute(gi), lru(gi))
        return carry

    lax.fori_loop(1, n_groups, steady, 0)
    _interleave(recur(n_groups - 1))


def _gdn_lru(zd, zb, conv_w, a_log, dt_bias, norm_g, lru_conv_w, lru_conv_b, wa, ba, wx, bx, lam):
    b, s, _ = zd.shape
    lane_vec = lambda p: jnp.zeros((1, LANES), F32).at[0, HEADS:2 * HEADS].set(p)
    ng = jnp.tile(norm_g, HEADS)[None, :]
    bd = lambda m: jax.scipy.linalg.block_diag(*[m[i] for i in range(m.shape[0])]).astype(BF16)
    vec = lambda v: v.reshape(1, GROUP_WIDTH)
    gdn_args = (conv_w, lane_vec(a_log), lane_vec(dt_bias), ng)
    lru_args = (lru_conv_w, vec(lru_conv_b), bd(wa), vec(ba), bd(wx), vec(bx), vec(lam))
    full = lambda a: pl.BlockSpec(a.shape, lambda i: (0,) * a.ndim)
    seq = lambda w: pl.BlockSpec((None, s, w), lambda i: (i, 0, 0))
    return pl.pallas_call(
        _gdn_lru_kernel,
        grid=(b,),
        in_specs=[seq(ZD_W)] + [full(a) for a in gdn_args] + [seq(ZB_W)] + [full(a) for a in lru_args],
        out_specs=[seq(GROUP_WIDTH), seq(GROUP_WIDTH)],
        out_shape=[jax.ShapeDtypeStruct((b, s, GROUP_WIDTH), F32)] * 2,
        scratch_shapes=[pltpu.VMEM((s, GROUP_WIDTH), F32),
                        pltpu.VMEM((s // CHUNK, 2 * CHUNK, GROUP_WIDTH), BF16),
                        pltpu.VMEM((s, GROUP_WIDTH), BF16),
                        pltpu.VMEM((HEADS, s, GDN_ROWS), BF16),
                        pltpu.VMEM((s // CHUNK, 1, GROUP_WIDTH), F32),
                        pltpu.VMEM((GROUP_WIDTH, GROUP_WIDTH), F32),
                        pltpu.VMEM((1, GROUP_WIDTH), F32)],
        compiler_params=pltpu.CompilerParams(dimension_semantics=("parallel",),
                                             vmem_limit_bytes=VMEM_LIMIT),
        name="gdn_lru",
    )(zd, *gdn_args, zb, *lru_args)


def _mix_out_kernel(x_ref, ya_ref, yb_ref, yc_ref, yd_ref, wo_ref, g_ref, wq_ref, x1_ref, q_ref):
    acc = x_ref[...]
    for gi, y_ref in enumerate((ya_ref, yb_ref, yc_ref, yd_ref)):
        acc = acc + jnp.dot(y_ref[...].astype(BF16), wo_ref[gi * GROUP_WIDTH:(gi + 1) * GROUP_WIDTH, :],
                            preferred_element_type=F32)
    x1_ref[...] = acc
    hq = (_rms(acc, D_MODEL) * g_ref[...]).astype(BF16)
    q_ref[...] = jnp.dot(hq, wq_ref[...], preferred_element_type=F32).astype(BF16)


def _mix_out(x2d, ys, w_out, gain, wq):
    t = x2d.shape[0]
    row = lambda w: pl.BlockSpec((ROW_TILE, w), lambda i: (i, 0))
    full = lambda a: pl.BlockSpec(a.shape, lambda i: (0,) * a.ndim)
    return pl.pallas_call(
        _mix_out_kernel,
        grid=(t // ROW_TILE,),
        in_specs=[row(D_MODEL)] + [row(GROUP_WIDTH)] * 4 + [full(w_out), full(gain), full(wq)],
        out_specs=[row(D_MODEL), row(D_MODEL)],
        out_shape=[jax.ShapeDtypeStruct((t, D_MODEL), F32), jax.ShapeDtypeStruct((t, D_MODEL), BF16)],
        compiler_params=pltpu.CompilerParams(dimension_semantics=("parallel",),
                                             vmem_limit_bytes=VMEM_LIMIT),
        name="mix_out",
    )(x2d, *ys, w_out, gain, wq)


def _mem_kv_kernel(mem_ref, g_ref, wkv_ref, k_ref, v_ref):
    m = (_rms(mem_ref[...], D_MODEL) * g_ref[...]).astype(BF16)
    k_ref[...] = jnp.dot(m, wkv_ref[:, :D_MODEL], preferred_element_type=F32).astype(BF16)
    v_ref[...] = jnp.dot(m, wkv_ref[:, D_MODEL:], preferred_element_type=F32).astype(BF16)


def _mem_kv(mem, gain, wkv):
    b, m, _ = mem.shape
    full = lambda a: pl.BlockSpec(a.shape, lambda i: (0,) * a.ndim)
    blk = pl.BlockSpec((None, m, D_MODEL), lambda i: (i, 0, 0))
    return pl.pallas_call(
        _mem_kv_kernel,
        grid=(b,),
        in_specs=[blk, full(gain), full(wkv)],
        out_specs=[blk, blk],
        out_shape=[jax.ShapeDtypeStruct((b, m, D_MODEL), BF16)] * 2,
        compiler_params=pltpu.CompilerParams(dimension_semantics=("parallel",),
                                             vmem_limit_bytes=VMEM_LIMIT),
        name="mem_kv",
    )(mem, gain, wkv)


def _xattn_mlp_kernel(x_ref, q_ref, k_ref, v_ref, wo_ref, g_ref, w1_ref, w2_ref, fg_ref, o_ref,
                      *, final_norm):
    x = x_ref[...]
    q = q_ref[...]
    heads = []
    for h in range(XA_HEADS):
        sl = slice(h * XA_HEAD_DIM, (h + 1) * XA_HEAD_DIM)
        sc = lax.dot_general(q[:, sl], k_ref[:, sl], (((1,), (1,)), ((), ())),
                             preferred_element_type=F32) * (XA_HEAD_DIM ** -0.5)
        e = jnp.exp(sc - jnp.max(sc, axis=-1, keepdims=True))
        p = e / jnp.sum(e, axis=-1, keepdims=True)
        heads.append(jnp.dot(p.astype(BF16), v_ref[:, sl], preferred_element_type=F32).astype(BF16))
    o = jnp.concatenate(heads, axis=1)
    x = x + jnp.dot(o, wo_ref[...], preferred_element_type=F32)
    hm = (_rms(x, D_MODEL) * g_ref[...]).astype(BF16)
    for c in range(D_FF // FF_SLAB):
        sl = slice(c * FF_SLAB, (c + 1) * FF_SLAB)
        a = jnp.maximum(jnp.dot(hm, w1_ref[:, sl], preferred_element_type=F32), 0.0)
        x = x + jnp.dot((a * a).astype(BF16), w2_ref[sl, :], preferred_element_type=F32)
    if final_norm:
        x = _rms(x, D_MODEL) * fg_ref[...]
    o_ref[...] = x


def _xattn_mlp(x1, q, k, v, wo, gain, w1, w2, final_gain, final_norm):
    b, s, _ = x1.shape
    m = k.shape[1]
    row = pl.BlockSpec((None, ROW_TILE, D_MODEL), lambda i, j: (i, j, 0))
    mem = pl.BlockSpec((None, m, D_MODEL), lambda i, j: (i, 0, 0))
    const = lambda a: pl.BlockSpec(a.shape, lambda i, j: (0,) * a.ndim, pipeline_mode=pl.Buffered(1))
    return pl.pallas_call(
        functools.partial(_xattn_mlp_kernel, final_norm=final_norm),
        grid=(b, s // ROW_TILE),
        in_specs=[row, row, mem, mem, const(wo), const(gain), const(w1), const(w2), const(final_gain)],
        out_specs=row,
        out_shape=jax.ShapeDtypeStruct((b, s, D_MODEL), F32),
        compiler_params=pltpu.CompilerParams(dimension_semantics=("parallel", "parallel"),
                                             vmem_limit_bytes=VMEM_LIMIT),
        name="xattn_mlp",
    )(x1, q, k, v, wo, gain, w1, w2, final_gain)


def _pack_w_in(w):
    d = w.shape[0]
    zeros = lambda n: jnp.zeros((d, n), w.dtype)
    c0 = ZA_W + ZB_W
    cq = w[:, c0:c0 + MLA_Q_LORA]
    ckv = w[:, c0 + MLA_Q_LORA:c0 + MLA_Q_LORA + MLA_KV_LORA]
    kr = w[:, c0 + MLA_Q_LORA + MLA_KV_LORA:c0 + MLA_Q_LORA + MLA_KV_LORA + MLA_ROPE]
    d0 = c0 + MLA_Q_LORA + MLA_KV_LORA + MLA_ROPE
    rest = w[:, d0:]
    packed = jnp.concatenate([
        w[:, :c0],
        cq, zeros(ZC_KV - MLA_Q_LORA), ckv, zeros(MLA_NOPE), kr, zeros(LANES - MLA_NOPE - MLA_ROPE),
        rest, zeros(ZD_W - rest.shape[1]),
    ], axis=1)
    assert packed.shape[1] == Z_W
    return packed.astype(BF16)


def kernel(x, mem, positions, norm_mix, w_in, ret_gn, lru_conv_w, lru_conv_b, lru_wa, lru_ba, lru_wx, lru_bx, lru_lambda, mla_q_norm, mla_w_uq, mla_kv_norm, mla_w_ukv, gdn_conv_w, gdn_a_log, gdn_dt_bias, gdn_norm, w_out, norm_xattn, norm_mem, xa_wq, xa_wkv, xa_wo, norm_mlp, mlp_w1, mlp_w2, final_norm):
    b, s, d = x.shape
    depth = w_in.shape[0]
    pos_f = positions.astype(F32)
    vec = lambda v: v.reshape(1, -1)
    for l in range(depth):
        za, zb, zc, zd = _in_proj(x.reshape(b * s, d), vec(norm_mix[l]), _pack_w_in(w_in[l]))
        seq = lambda z: z.reshape(b, s, z.shape[-1])
        y_a = _retention(seq(za), pos_f, ret_gn[l])
        y_c = _mla(seq(zc), pos_f, mla_q_norm[l], mla_w_uq[l], mla_kv_norm[l], mla_w_ukv[l])
        y_d, y_b = _gdn_lru(seq(zd), seq(zb), gdn_conv_w[l], gdn_a_log[l], gdn_dt_bias[l], gdn_norm[l],
                            lru_conv_w[l], lru_conv_b[l], lru_wa[l], lru_ba[l], lru_wx[l], lru_bx[l],
                            lru_lambda[l])
        ys = [y.reshape(b * s, GROUP_WIDTH) for y in (y_a, y_b, y_c, y_d)]
        x1, q = _mix_out(x.reshape(b * s, d), ys, w_out[l].astype(BF16), vec(norm_xattn[l]),
                         xa_wq[l].astype(BF16))
        k, v = _mem_kv(mem, vec(norm_mem[l]), xa_wkv[l].astype(BF16))
        x = _xattn_mlp(x1.reshape(b, s, d), q.reshape(b, s, d), k, v, xa_wo[l].astype(BF16),
                       vec(norm_mlp[l]), mlp_w1[l].astype(BF16), mlp_w2[l].astype(BF16),
                       vec(final_norm), l == depth - 1)
    return x
```

```python
import functools

import jax
import jax.numpy as jnp
from jax import lax
from jax.experimental import pallas as pl
from jax.experimental.pallas import tpu as pltpu

F32 = jnp.float32
BF16 = jnp.bfloat16

D_MODEL = 1024
CHUNK = 64
HEADS = 4
HEAD_DIM = 64
GROUP_WIDTH = HEADS * HEAD_DIM
EPS = 1e-6
ROPE_THETA = 10000.0
LRU_C = 8.0
MLA_Q_LORA = 192
MLA_KV_LORA = 128
MLA_ROPE = 32
MLA_NOPE = 64
XA_HEADS = 4
XA_HEAD_DIM = D_MODEL // XA_HEADS
D_FF = 4 * D_MODEL
FF_SLAB = 1024
ATT_TILE = 512
ROPE_TILE = 256
GDN_ROWS = 256
SOLVE_BASE = 16
assert GDN_ROWS == GROUP_WIDTH and HEAD_DIM == CHUNK
ROW_TILE = 512
TOKEN_BLOCK = 512
TOKEN_SUB = 256
SUBLANES = 8
LANES = 128
VMEM_LIMIT = 56 * 1024 * 1024

ZA_W = 4 * GROUP_WIDTH
ZB_W = 2 * GROUP_WIDTH
ZC_W = 512
ZD_W = 1152
ZC_KV = 256
ZC_KR = 384
Z_W = ZA_W + ZB_W + ZC_W + ZD_W


def _dot(a, b):
    return jnp.dot(a.astype(BF16), b.astype(BF16), preferred_element_type=F32)


def _dot_nt(a, b):
    return lax.dot_general(a.astype(BF16), b.astype(BF16), (((1,), (1,)), ((), ())),
                           preferred_element_type=F32)


def _dot_tn(a, b):
    return lax.dot_general(a.astype(BF16), b.astype(BF16), (((0,), (0,)), ((), ())),
                           preferred_element_type=F32)


def _split3(x):
    x1 = x.astype(BF16)
    r1 = x - x1.astype(F32)
    x2 = r1.astype(BF16)
    x3 = (r1 - x2.astype(F32)).astype(BF16)
    return x1, x2, x3


def _dot3(a, b):
    a1 = a.astype(BF16)
    a2 = (a - a1.astype(F32)).astype(BF16)
    b1 = b.astype(BF16)
    b2 = (b - b1.astype(F32)).astype(BF16)
    d = functools.partial(jnp.dot, preferred_element_type=F32)
    return d(a1, b1) + (d(a1, b2) + d(a2, b1))


def _silu(x):
    return x * jax.nn.sigmoid(x)


def _softplus(x):
    return jnp.maximum(x, 0.0) + jnp.log1p(jnp.exp(-jnp.abs(x)))


def _expm1(y):
    e = jnp.exp(y)
    near = (y > -0.5) & (e != 1.0)
    safe_log = jnp.where(near, jnp.log(e), 1.0)
    return jnp.where(near, (e - 1.0) * y / safe_log, jnp.where(e == 1.0, y, e - 1.0))


def _rms(x, n):
    return x * lax.rsqrt(jnp.sum(x * x, axis=-1, keepdims=True) * (1.0 / n) + EPS)


def _chunk_rows(n):
    return pl.ds(pl.multiple_of(n * CHUNK, CHUNK), CHUNK)


def _causal_conv_rows(z_ref, n, nrows, lo, hi, w):
    cur = z_ref[pl.ds(pl.multiple_of(n * nrows, nrows), nrows), lo:hi]
    prev_start = pl.multiple_of(jnp.maximum(n * nrows - SUBLANES, 0), SUBLANES)
    prev = z_ref[pl.ds(prev_start, SUBLANES), lo:hi]
    prev = jnp.where(n > 0, prev, 0.0)
    tile = jnp.concatenate([prev, cur], axis=0)
    acc = cur * w[3:4, :]
    for k in (1, 2, 3):
        acc = acc + pltpu.roll(tile, k, 0)[SUBLANES:, :] * w[3 - k:4 - k, :]
    return acc


def _lane_block_swap(x, half, first):
    return jnp.where(first, pltpu.roll(x, LANES - half, 1), pltpu.roll(x, half, 1))


def _in_proj_kernel(x_ref, g_ref, w_ref, za_ref, zb_ref, zc_ref, zd_ref):
    x = x_ref[...]
    h = (_rms(x, D_MODEL) * g_ref[...]).astype(BF16)
    lo = 0
    for ref in (za_ref, zb_ref, zc_ref, zd_ref):
        hi = lo + ref.shape[-1]
        ref[...] = jnp.dot(h, w_ref[:, lo:hi], preferred_element_type=F32)
        lo = hi


def _in_proj(x2d, gain, w_packed):
    t = x2d.shape[0]
    row = lambda w: pl.BlockSpec((ROW_TILE, w), lambda i: (i, 0))
    full = lambda a: pl.BlockSpec(a.shape, lambda i: (0,) * a.ndim)
    return pl.pallas_call(
        _in_proj_kernel,
        grid=(t // ROW_TILE,),
        in_specs=[row(D_MODEL), full(gain), full(w_packed)],
        out_specs=[row(ZA_W), row(ZB_W), row(ZC_W), row(ZD_W)],
        out_shape=[jax.ShapeDtypeStruct((t, w), F32) for w in (ZA_W, ZB_W, ZC_W, ZD_W)],
        compiler_params=pltpu.CompilerParams(dimension_semantics=("parallel",),
                                             vmem_limit_bytes=VMEM_LIMIT),
        name="in_proj",
    )(x2d, gain, w_packed)


def _compact_positions(pos_f, n_freq):
    b, s = pos_f.shape
    nb = LANES // n_freq
    r = ROPE_TILE // nb
    p = pos_f.reshape(b, s // ROPE_TILE, nb, r).transpose(0, 1, 3, 2)
    return jnp.repeat(p, n_freq, axis=-1)


def _rope_expanders(n_freq, width, rope_lo, rope_hi):
    nb = LANES // n_freq
    src = jnp.arange(LANES)
    lane = jnp.arange(width)
    on_rope = (lane >= rope_lo) & (lane < rope_hi)
    hit = (src[:, None] % n_freq == lane[None, :] % n_freq) & on_rope[None, :]
    return jnp.stack([(hit & (src[:, None] // n_freq == r)) for r in range(nb)]).astype(BF16)


def _expand_rope_table(t_c, e_ref):
    return jnp.concatenate([_dot_exact(t_c, e_ref[r], 3) for r in range(e_ref.shape[0])], axis=0)


def _retention_stages(z_ref, posc_ref, invf_ref, e_ref, dmat_ref, xi_ref, zeta_ref, cd_ref, gn_ref,
                      o_ref, state_ref):
    gw = GROUP_WIDTH
    n_rows = ROPE_TILE
    lane = lax.broadcasted_iota(jnp.int32, (n_rows, LANES), 1)
    first = (lane % HEAD_DIM) < (HEAD_DIM // 2)
    lane_wide = lax.broadcasted_iota(jnp.int32, (n_rows, gw), 1)
    first_wide = (lane_wide % HEAD_DIM) < (HEAD_DIM // 2)
    ri = lax.broadcasted_iota(jnp.int32, (gw, gw), 0)
    ci = lax.broadcasted_iota(jnp.int32, (gw, gw), 1)
    same_head = (ri // HEAD_DIM) == (ci // HEAD_DIM)
    ones_bd = same_head.astype(BF16)
    state_ref[...] = jnp.zeros_like(state_ref)

    def rope(x, cos, sin_signed):
        parts = []
        for c in range(gw // LANES):
            sl = slice(c * LANES, (c + 1) * LANES)
            xs = _lane_block_swap(x[:, sl], HEAD_DIM // 2, first)
            parts.append(x[:, sl] * cos[:, sl] + xs * sin_signed[:, sl])
        return jnp.concatenate(parts, axis=1)

    def body(g):
        rows = pl.ds(pl.multiple_of(g * n_rows, n_rows), n_rows)
        ang = posc_ref[g] * invf_ref[...]
        cos = _expand_rope_table(jnp.cos(ang), e_ref)
        sin = _expand_rope_table(jnp.sin(ang), e_ref)
        yield
        sin_signed = jnp.where(first_wide, -sin, sin)
        q = (rope(z_ref[rows, 0:gw], cos, sin_signed) * (HEAD_DIM ** -0.5)).astype(BF16)
        k = rope(z_ref[rows, gw:2 * gw], cos, sin_signed)
        v = z_ref[rows, 2 * gw:3 * gw].astype(BF16)
        kz = (k * zeta_ref[...]).astype(BF16)
        k = k.astype(BF16)
        heads = [slice(h * HEAD_DIM, (h + 1) * HEAD_DIM) for h in range(HEADS)]
        blocks = [slice(c * CHUNK, (c + 1) * CHUNK) for c in range(n_rows // CHUNK)]
        scores = [_dot_nt(q[:, sl], k[:, sl]) for sl in heads]
        yield
        kvs = [_dot_tn(kz[blk], v[blk]) for blk in blocks]
        yield
        intra = [_dot(scores[h] * dmat_ref[h], v[:, sl])
                 for h, sl in enumerate(heads)]
        yield
        st = state_ref[...]
        cross = []
        for c, blk in enumerate(blocks):
            cross.append(_dot(q[blk], st))
            st = st * cd_ref[...] + jnp.where(same_head, kvs[c], 0.0)
        state_ref[...] = st
        yield
        o = jnp.concatenate(intra, axis=1) + jnp.concatenate(cross, axis=0) * xi_ref[...]
        mu = _dot_exact(o, ones_bd, 2) * (1.0 / HEAD_DIM)
        yield
        d = o - mu
        var = _dot_exact(d * d, ones_bd, 2) * (1.0 / HEAD_DIM)
        gate = z_ref[rows, 3 * gw:4 * gw]
        o_ref[rows, :] = d * lax.rsqrt(var + EPS) * gn_ref[...] * _silu(gate)

    return body


def _retention_consts(pos_f, ret_gn):
    h = HEADS
    half = HEAD_DIM // 2
    per = ROPE_TILE // CHUNK
    inv_freq = jnp.power(ROPE_THETA, -jnp.arange(half, dtype=F32) / half)
    invf = jnp.tile(inv_freq, LANES // half)[None, :]
    posc = _compact_positions(pos_f, half)
    expand = _rope_expanders(half, GROUP_WIDTH, 0, GROUP_WIDTH)
    log_gamma = jnp.log1p(-jnp.exp2(-5.0 - jnp.arange(h, dtype=F32)))
    idx = jnp.arange(CHUNK, dtype=F32)
    dmat = jnp.exp(jnp.abs(idx[:, None] - idx[None, :])[None] * log_gamma[:, None, None])
    dmat = jnp.stack([jnp.kron(jnp.eye(per, dtype=F32), dmat[i]) for i in range(h)])
    xi = jnp.repeat(jnp.exp((idx[:, None] + 1.0) * log_gamma), HEAD_DIM, axis=1)
    zeta = jnp.repeat(jnp.exp((CHUNK - 1.0 - idx)[:, None] * log_gamma), HEAD_DIM, axis=1)
    xi = jnp.tile(xi, (per, 1))
    zeta = jnp.tile(zeta, (per, 1))
    cd = jnp.repeat(jnp.exp(CHUNK * log_gamma), HEAD_DIM)[None, :]
    gn = ret_gn.reshape(1, GROUP_WIDTH)
    return posc, (invf, expand, dmat, xi, zeta, cd, gn)


def _interleave(*stage_generators):
    live = list(stage_generators)
    while live:
        for gen in list(live):
            try:
                next(gen)
            except StopIteration:
                live.remove(gen)


def _lru_stages(gi, z_ref, cw_ref, cb_ref, wa_ref, ba_ref, wx_ref, bx_ref, lam_ref, o_ref, h_ref):
    w = GROUP_WIDTH
    row = lax.broadcasted_iota(jnp.int32, (CHUNK, w), 0)
    for c in range(GDN_ROWS // CHUNK):
        n = gi * (GDN_ROWS // CHUNK) + c
        rows = _chunk_rows(n)
        xc = _causal_conv_rows(z_ref, n, CHUNK, 0, w, cw_ref[...]) + cb_ref[...]
        r = jax.nn.sigmoid(_dot(xc, wa_ref[...]) + ba_ref[...])
        i = jax.nn.sigmoid(_dot(xc, wx_ref[...]) + bx_ref[...])
        log_a = -LRU_C * r * _softplus(-lam_ref[...])
        a = jnp.exp(log_a)
        u = jnp.sqrt(-_expm1(2.0 * log_a)) * (i * xc)
        d = 1
        while d < CHUNK:
            a_sh = jnp.where(row >= d, pltpu.roll(a, d, 0), 1.0)
            u_sh = jnp.where(row >= d, pltpu.roll(u, d, 0), 0.0)
            u = a * u_sh + u
            a = a * a_sh
            d *= 2
        hs = u + a * h_ref[...]
        h_ref[...] = hs[CHUNK - 1:CHUNK, :]
        gate = z_ref[rows, w:2 * w]
        o_ref[rows, :] = hs * jax.nn.gelu(gate, approximate=True)
        yield


def _mla_parts(z_ref, posc_ref, invf_ref, e_ref, base_ref, qn_ref, wq_ref, kvn_ref, wk_ref, wvt_ref,
               o_ref, q_s, k_s, vt_s):
    s = z_ref.shape[0]
    nt = s // ATT_TILE
    scale = (MLA_NOPE + MLA_ROPE) ** -0.5
    lane = lax.broadcasted_iota(jnp.int32, (ROPE_TILE, LANES), 1)
    first = (lane >= MLA_NOPE) & (lane < MLA_NOPE + MLA_ROPE // 2)

    def rope(x, cos, sin_signed):
        return x * cos + _lane_block_swap(x, MLA_ROPE // 2, first) * sin_signed

    def project(t):
        rows = pl.ds(pl.multiple_of(t * ROPE_TILE, ROPE_TILE), ROPE_TILE)
        ang = posc_ref[t] * invf_ref[...]
        cos = _expand_rope_table(jnp.cos(ang), e_ref) + base_ref[...]
        sin = _expand_rope_table(jnp.sin(ang), e_ref)
        yield
        sin_signed = jnp.where(first, -sin, sin)
        cq = (_rms(z_ref[rows, 0:ZC_KV], MLA_Q_LORA) * qn_ref[...]).astype(BF16)
        ckv_f = _rms(z_ref[rows, ZC_KV:ZC_KR], MLA_KV_LORA) * kvn_ref[...]
        ckv = ckv_f.astype(BF16)
        k_rope = rope(z_ref[rows, ZC_KR:ZC_W], cos, sin_signed)
        vt = jnp.dot(wvt_ref[...], ckv_f.T.astype(BF16), preferred_element_type=F32)
        qs = [jnp.dot(cq, wq_ref[h], preferred_element_type=F32) for h in range(HEADS)]
        yield
        ks = [jnp.dot(ckv, wk_ref[h], preferred_element_type=F32) for h in range(HEADS)]
        vt_s[:, rows] = vt.astype(BF16)
        yield
        for h in range(HEADS):
            q_s[h, rows, :] = rope(qs[h], cos, sin_signed).astype(BF16)
            k_s[h, rows, :] = (ks[h] + k_rope).astype(BF16)

    key_id = lax.broadcasted_iota(jnp.int32, (ATT_TILE, ATT_TILE), 0) // CHUNK
    qry_id = lax.broadcasted_iota(jnp.int32, (ATT_TILE, ATT_TILE), 1) // CHUNK
    diag_visible = key_id <= qry_id

    def q_tile(i, carry):
        rows = pl.ds(pl.multiple_of(i * ATT_TILE, ATT_TILE), ATT_TILE)

        def kv_tile(j, state, on_diagonal):
            cols = pl.ds(pl.multiple_of(j * ATT_TILE, ATT_TILE), ATT_TILE)
            scores = [lax.dot_general(k_s[h, cols, :], q_s[h, rows, :], (((1,), (1,)), ((), ())),
                                      preferred_element_type=F32) for h in range(HEADS)]
            new_state = []
            for h in range(HEADS):
                m, l, acc = state[h]
                sc = scores[h] * scale
                if on_diagonal:
                    sc = jnp.where(diag_visible, sc, -jnp.inf)
                m_new = jnp.maximum(m, jnp.max(sc, axis=0, keepdims=True))
                p = jnp.exp(sc - m_new)
                alpha = jnp.exp(m - m_new)
                l = alpha * l + jnp.sum(p, axis=0, keepdims=True)
                vt = vt_s[h * HEAD_DIM:(h + 1) * HEAD_DIM, cols]
                acc = alpha * acc + jnp.dot(vt, p.astype(BF16), preferred_element_type=F32)
                new_state.append((m_new, l, acc))
            return tuple(new_state)

        init = tuple((jnp.full((1, ATT_TILE), -jnp.inf, F32), jnp.zeros((1, ATT_TILE), F32),
                      jnp.zeros((HEAD_DIM, ATT_TILE), F32)) for _ in range(HEADS))
        state = lax.fori_loop(0, i, lambda j, st: kv_tile(j, st, False), init)
        state = kv_tile(i, state, True)
        o_t = jnp.concatenate([acc / l for _, l, acc in state], axis=0)
        o_ref[rows, :] = o_t.T
        return carry

    def attention():
        lax.fori_loop(0, nt, q_tile, 0)

    return project, attention


def _mla_consts(pos_f, q_norm, w_uq, kv_norm, w_ukv):
    h = HEADS
    half = MLA_ROPE // 2
    inv_freq = jnp.power(ROPE_THETA, -jnp.arange(half, dtype=F32) / half)
    invf = jnp.tile(inv_freq, LANES // half)[None, :]
    posc = _compact_positions(pos_f, half)
    expand = _rope_expanders(half, LANES, MLA_NOPE, MLA_NOPE + MLA_ROPE)
    lane = jnp.arange(LANES)
    base = ((lane < MLA_NOPE) | (lane >= MLA_NOPE + MLA_ROPE)).astype(F32)[None, :]
    qn = jnp.zeros((1, ZC_KV), F32).at[0, :MLA_Q_LORA].set(q_norm)
    kvn = kv_norm.reshape(1, MLA_KV_LORA)
    wq = w_uq.reshape(MLA_Q_LORA, h, MLA_NOPE + MLA_ROPE).transpose(1, 0, 2)
    wq = jnp.pad(wq, ((0, 0), (0, ZC_KV - MLA_Q_LORA), (0, LANES - MLA_NOPE - MLA_ROPE))).astype(BF16)
    wkv = w_ukv.reshape(MLA_KV_LORA, h, MLA_NOPE + HEAD_DIM)
    wk = jnp.pad(wkv[:, :, :MLA_NOPE].transpose(1, 0, 2),
                 ((0, 0), (0, 0), (0, LANES - MLA_NOPE))).astype(BF16)
    wvt = wkv[:, :, MLA_NOPE:].reshape(MLA_KV_LORA, h * HEAD_DIM).T.astype(BF16)
    return posc, (invf, expand, base, qn, wq, kvn, wk, wvt)


N_RET_CONSTS = 7
N_MLA_CONSTS = 8


def _ret_mla_kernel(*refs):
    za_ref, pa_ref = refs[0:2]
    ret_consts = refs[2:2 + N_RET_CONSTS]
    zc_ref, pc_ref = refs[2 + N_RET_CONSTS:4 + N_RET_CONSTS]
    mla_consts = refs[4 + N_RET_CONSTS:4 + N_RET_CONSTS + N_MLA_CONSTS]
    oa_ref, oc_ref, state_ref, q_s, k_s, vt_s = refs[4 + N_RET_CONSTS + N_MLA_CONSTS:]
    retention = _retention_stages(za_ref, pa_ref, *ret_consts, oa_ref, state_ref)
    project, attention = _mla_parts(zc_ref, pc_ref, *mla_consts, oc_ref, q_s, k_s, vt_s)

    def group(g, carry):
        _interleave(retention(g), project(g))
        return carry

    lax.fori_loop(0, za_ref.shape[0] // ROPE_TILE, group, 0)
    attention()


def _ret_mla(za, zc, pos_f, ret_gn, q_norm, w_uq, kv_norm, w_ukv):
    b, s, _ = za.shape
    pos_a, ret_consts = _retention_consts(pos_f, ret_gn)
    pos_c, mla_consts = _mla_consts(pos_f, q_norm, w_uq, kv_norm, w_ukv)
    assert len(ret_consts) == N_RET_CONSTS and len(mla_consts) == N_MLA_CONSTS
    full = lambda a: pl.BlockSpec(a.shape, lambda i: (0,) * a.ndim)
    seq = lambda w: pl.BlockSpec((None, s, w), lambda i: (i, 0, 0))
    tiles = lambda p: pl.BlockSpec((None,) + p.shape[1:], lambda i: (i, 0, 0, 0))
    return pl.pallas_call(
        _ret_mla_kernel,
        grid=(b,),
        in_specs=[seq(ZA_W), tiles(pos_a)] + [full(a) for a in ret_consts]
                 + [seq(ZC_W), tiles(pos_c)] + [full(a) for a in mla_consts],
        out_specs=[seq(GROUP_WIDTH), seq(GROUP_WIDTH)],
        out_shape=[jax.ShapeDtypeStruct((b, s, GROUP_WIDTH), F32)] * 2,
        scratch_shapes=[pltpu.VMEM((GROUP_WIDTH, GROUP_WIDTH), F32),
                        pltpu.VMEM((HEADS, s, LANES), BF16), pltpu.VMEM((HEADS, s, LANES), BF16),
                        pltpu.VMEM((GROUP_WIDTH, s), BF16)],
        compiler_params=pltpu.CompilerParams(dimension_semantics=("parallel",),
                                             vmem_limit_bytes=VMEM_LIMIT),
        name="ret_mla",
    )(za, pos_a, *ret_consts, zc, pos_c, *mla_consts)


def _bf16_parts(x, parts):
    out = []
    for _ in range(parts):
        p = x.astype(BF16)
        out.append(p)
        x = x - p.astype(F32)
    return out


def _dot_exact(a, b, parts=None):
    f32_is_lhs = isinstance(a, list) or a.dtype == F32
    split = a if f32_is_lhs else b
    if not isinstance(split, list):
        split = _bf16_parts(split, parts)
    terms = [jnp.dot(p, b, preferred_element_type=F32) if f32_is_lhs
             else jnp.dot(a, p, preferred_element_type=F32) for p in split]
    acc = terms[-1]
    for t in terms[-2::-1]:
        acc = acc + t
    return acc


def _gdn_lru_kernel(z_ref, cw_ref, alog_ref, dtb_ref, ng_ref,
                    zb_ref, lcw_ref, lcb_ref, lwa_ref, lba_ref, lwx_ref, lbx_ref, llam_ref,
                    o_ref, ob_ref, u_s, wq_s, kd_s, attn_s, gl_s, state_ref, h_ref):
    s = z_ref.shape[0]
    gw = GROUP_WIDTH
    g_rows = GDN_ROWS
    per = g_rows // CHUNK
    ri = lax.broadcasted_iota(jnp.int32, (g_rows, g_rows), 0)
    ci = lax.broadcasted_iota(jnp.int32, (g_rows, g_rows), 1)
    block_mask = lambda width: (ri // width) == (ci // width)
    eye = ri == ci
    same_block = block_mask(CHUNK)
    causal = same_block & (ri >= ci)
    strict = same_block & (ri > ci)
    tri = causal.astype(BF16)
    ones_bd = same_block.astype(BF16)
    er = lax.broadcasted_iota(jnp.int32, (LANES, gw), 0)
    ec = lax.broadcasted_iota(jnp.int32, (LANES, gw), 1) // HEAD_DIM
    expand_beta = (er == ec).astype(BF16)
    expand_alpha = (er == ec + HEADS).astype(BF16)

    def precompute(gi):
        rows = pl.ds(pl.multiple_of(gi * g_rows, g_rows), g_rows)
        qkv = _silu(_causal_conv_rows(z_ref, gi, g_rows, 0, 3 * gw, cw_ref[...]))
        q = qkv[:, 0:gw]
        k = qkv[:, gw:2 * gw]
        v = qkv[:, 2 * gw:3 * gw]
        small = z_ref[rows, 4 * gw:4 * gw + LANES]
        beta_t = jax.nn.sigmoid(small)
        g_t = -jnp.exp(alog_ref[...]) * _softplus(small + dtb_ref[...])
        qn = q * lax.rsqrt(_dot_exact(q * q, ones_bd, 2) + EPS) * (HEAD_DIM ** -0.5)
        kn = k * lax.rsqrt(_dot_exact(k * k, ones_bd, 2) + EPS)
        beta_w = _dot_exact(beta_t, expand_beta, 3)
        g_t_parts = _bf16_parts(g_t, 3)
        g_w = _dot_exact(g_t_parts, expand_alpha)
        gc_t = _dot_exact(tri, g_t_parts)
        yield
        g_w_parts = _bf16_parts(g_w, 3)
        gc_w = _dot_exact(tri, g_w_parts)
        gl_w = _dot_exact(ones_bd, g_w_parts)
        gc_tt = gc_t.T
        kb_w = kn * beta_w
        vb_w = v * beta_w
        heads = [slice(h * HEAD_DIM, (h + 1) * HEAD_DIM) for h in range(HEADS)]
        kh = [kn[:, sl].astype(BF16) for sl in heads]
        grams = [_dot_nt(kb_w[:, sl], kh[h]) for h, sl in enumerate(heads)]
        yield
        qks = [_dot_nt(qn[:, sl], kh[h]) for h, sl in enumerate(heads)]
        egc_w = jnp.exp(gc_w)
        kbe_w = kb_w * egc_w
        yield
        lowers, rhs = [], []
        for h, sl in enumerate(heads):
            gcol = jnp.broadcast_to(gc_t[:, HEADS + h:HEADS + h + 1], (g_rows, g_rows))
            grow = jnp.broadcast_to(gc_tt[HEADS + h:HEADS + h + 1, :], (g_rows, g_rows))
            decay = jnp.exp(jnp.where(causal, gcol - grow, -jnp.inf))
            lowers.append(jnp.where(strict, grams[h] * decay, 0.0))
            attn_s[h, rows, :] = (qks[h] * decay).astype(BF16)
            rhs.append(jnp.concatenate([vb_w[:, sl], kbe_w[:, sl]], axis=1))
        bf = lambda xs: [x.astype(BF16) for x in xs]
        plus_eye = lambda x: jnp.where(eye, x + 1.0, x)
        inner = block_mask(SOLVE_BASE)
        d = [jnp.where(inner, low, 0.0) for low in lowers]
        factors = [[jnp.where(eye, 1.0, -x) for x in d]]
        power = bf(d)
        width = 2
        while width < SOLVE_BASE:
            power_f = [_dot(p, p) for p in power]
            yield
            factors.append([plus_eye(x) for x in power_f])
            power = bf(power_f)
            width *= 2
        while len(factors) > 1:
            factors = [[_dot(x, y) for x, y in zip(factors[i], factors[i + 1])] if i + 1 < len(factors)
                       else factors[i] for i in range(0, len(factors), 2)]
            yield
        t = factors[0]
        width = SOLVE_BASE
        while width < CHUNK:
            outer = block_mask(2 * width)
            e = [jnp.where(outer & ~inner, low, 0.0).astype(BF16) for low in lowers]
            t_b = bf(t)
            et = [_dot(x, y) for x, y in zip(e, t_b)]
            yield
            t = [x - _dot(y, z) for x, y, z in zip(t, t_b, et)]
            yield
            inner = outer
            width *= 2
        sols = [_dot(x, y) for x, y in zip(t, rhs)]
        yield
        u_s[rows, :] = jnp.concatenate([sol[:, :HEAD_DIM] for sol in sols], axis=1)
        w_all = jnp.concatenate([sol[:, HEAD_DIM:] for sol in sols], axis=1).astype(BF16)
        qd_w = (qn * egc_w).astype(BF16)
        kd_s[rows, :] = (kn * jnp.exp(gl_w - gc_w)).astype(BF16)
        for c in range(per):
            blk = slice(c * CHUNK, (c + 1) * CHUNK)
            wq_s[gi * per + c, 0:CHUNK, :] = w_all[blk]
            wq_s[gi * per + c, CHUNK:2 * CHUNK, :] = qd_w[blk]
            gl_s[gi * per + c] = gl_w[c * CHUNK:c * CHUNK + 1, :]

    def recur(gi):
        st = state_ref[...]
        outs = []
        for c in range(per):
            n = gi * per + c
            rows = _chunk_rows(n)
            r1 = _dot(wq_s[n], st)
            yield
            v_new = (u_s[rows, :] - r1[0:CHUNK]).astype(BF16)
            ds = lax.dot_general(kd_s[rows, :], v_new, (((0,), (0,)), ((), ())),
                                 preferred_element_type=F32)
            parts = []
            for h in range(HEADS):
                sl = slice(h * HEAD_DIM, (h + 1) * HEAD_DIM)
                a = attn_s[h, rows, c * CHUNK:(c + 1) * CHUNK]
                parts.append(jnp.dot(a, v_new[:, sl], preferred_element_type=F32))
            yield
            outs.append(r1[CHUNK:2 * CHUNK] + jnp.concatenate(parts, axis=1))
            st = st * jnp.exp(gl_s[n]) + jnp.where(same_block, ds, 0.0)
        state_ref[...] = st
        rows = pl.ds(pl.multiple_of(gi * g_rows, g_rows), g_rows)
        o = jnp.concatenate(outs, axis=0)
        ms = _dot_exact(o * o, ones_bd, 2) * (1.0 / HEAD_DIM)
        gate = z_ref[rows, 3 * gw:4 * gw]
        o_ref[rows, :] = o * lax.rsqrt(ms + EPS) * ng_ref[...] * _silu(gate)

    def lru(gi):
        return _lru_stages(gi, zb_ref, lcw_ref, lcb_ref, lwa_ref, lba_ref, lwx_ref, lbx_ref, llam_ref,
                           ob_ref, h_ref)

    state_ref[...] = jnp.zeros_like(state_ref)
    h_ref[...] = jnp.zeros_like(h_ref)
    n_groups = s // g_rows
    _interleave(precompute(0), lru(0))

    def steady(gi, carry):
        _interleave(recur(gi - 1), precompute(gi), lru(gi))
        return carry

    lax.fori_loop(1, n_groups, steady, 0)
    _interleave(recur(n_groups - 1))


def _gdn_lru(zd, zb, conv_w, a_log, dt_bias, norm_g, lru_conv_w, lru_conv_b, wa, ba, wx, bx, lam):
    b, s, _ = zd.shape
    lane_vec = lambda p: jnp.zeros((1, LANES), F32).at[0, HEADS:2 * HEADS].set(p)
    ng = jnp.tile(norm_g, HEADS)[None, :]
    bd = lambda m: jax.scipy.linalg.block_diag(*[m[i] for i in range(m.shape[0])]).astype(BF16)
    vec = lambda v: v.reshape(1, GROUP_WIDTH)
    gdn_args = (conv_w, lane_vec(a_log), lane_vec(dt_bias), ng)
    lru_args = (lru_conv_w, vec(lru_conv_b), bd(wa), vec(ba), bd(wx), vec(bx), vec(lam))
    full = lambda a: pl.BlockSpec(a.shape, lambda i: (0,) * a.ndim)
    seq = lambda w: pl.BlockSpec((None, s, w), lambda i: (i, 0, 0))
    return pl.pallas_call(
        _gdn_lru_kernel,
        grid=(b,),
        in_specs=[seq(ZD_W)] + [full(a) for a in gdn_args] + [seq(ZB_W)] + [full(a) for a in lru_args],
        out_specs=[seq(GROUP_WIDTH), seq(GROUP_WIDTH)],
        out_shape=[jax.ShapeDtypeStruct((b, s, GROUP_WIDTH), F32)] * 2,
        scratch_shapes=[pltpu.VMEM((s, GROUP_WIDTH), F32),
                        pltpu.VMEM((s // CHUNK, 2 * CHUNK, GROUP_WIDTH), BF16),
                        pltpu.VMEM((s, GROUP_WIDTH), BF16),
                        pltpu.VMEM((HEADS, s, GDN_ROWS), BF16),
                        pltpu.VMEM((s // CHUNK, 1, GROUP_WIDTH), F32),
                        pltpu.VMEM((GROUP_WIDTH, GROUP_WIDTH), F32),
                        pltpu.VMEM((1, GROUP_WIDTH), F32)],
        compiler_params=pltpu.CompilerParams(dimension_semantics=("parallel",),
                                             vmem_limit_bytes=VMEM_LIMIT),
        name="gdn_lru",
    )(zd, *gdn_args, zb, *lru_args)


def _mem_kv_kernel(mem_ref, g_ref, wkv_ref, k_ref, v_ref):
    m = (_rms(mem_ref[...], D_MODEL) * g_ref[...]).astype(BF16)
    k_ref[...] = jnp.dot(m, wkv_ref[:, :D_MODEL], preferred_element_type=F32).astype(BF16)
    v_ref[...] = jnp.dot(m, wkv_ref[:, D_MODEL:], preferred_element_type=F32).astype(BF16)


def _mem_kv(mem, gain, wkv):
    b, m, _ = mem.shape
    full = lambda a: pl.BlockSpec(a.shape, lambda i: (0,) * a.ndim)
    blk = pl.BlockSpec((None, m, D_MODEL), lambda i: (i, 0, 0))
    return pl.pallas_call(
        _mem_kv_kernel,
        grid=(b,),
        in_specs=[blk, full(gain), full(wkv)],
        out_specs=[blk, blk],
        out_shape=[jax.ShapeDtypeStruct((b, m, D_MODEL), BF16)] * 2,
        compiler_params=pltpu.CompilerParams(dimension_semantics=("parallel",),
                                             vmem_limit_bytes=VMEM_LIMIT),
        name="mem_kv",
    )(mem, gain, wkv)


def _token_kernel(x_ref, ya_ref, yb_ref, yc_ref, yd_ref, k_ref, v_ref, wmix_ref, gx_ref, wq_ref, wo_ref,
                  gm_ref, w1_ref, w2_ref, fg_ref, o_ref, *, final_norm):
    def part(rs):
        x = x_ref[rs, :]
        for gi, y_ref in enumerate((ya_ref, yb_ref, yc_ref, yd_ref)):
            x = x + jnp.dot(y_ref[rs, :].astype(BF16), wmix_ref[gi * GROUP_WIDTH:(gi + 1) * GROUP_WIDTH, :],
                            preferred_element_type=F32)
        yield
        hq = (_rms(x, D_MODEL) * gx_ref[...]).astype(BF16)
        q = jnp.dot(hq, wq_ref[...], preferred_element_type=F32).astype(BF16)
        yield
        heads = [slice(h * XA_HEAD_DIM, (h + 1) * XA_HEAD_DIM) for h in range(XA_HEADS)]
        scores = [lax.dot_general(q[:, sl], k_ref[:, sl], (((1,), (1,)), ((), ())),
                                  preferred_element_type=F32) for sl in heads]
        yield
        outs = []
        for h, sl in enumerate(heads):
            sc = scores[h] * (XA_HEAD_DIM ** -0.5)
            e = jnp.exp(sc - jnp.max(sc, axis=-1, keepdims=True))
            p = e / jnp.sum(e, axis=-1, keepdims=True)
            outs.append(jnp.dot(p.astype(BF16), v_ref[:, sl], preferred_element_type=F32).astype(BF16))
        yield
        x = x + jnp.dot(jnp.concatenate(outs, axis=1), wo_ref[...], preferred_element_type=F32)
        yield
        hm = (_rms(x, D_MODEL) * gm_ref[...]).astype(BF16)
        slabs = [slice(c * FF_SLAB, (c + 1) * FF_SLAB) for c in range(D_FF // FF_SLAB)]
        a = jnp.dot(hm, w1_ref[:, slabs[0]], preferred_element_type=F32)
        yield
        for c in range(1, len(slabs)):
            a_next = jnp.dot(hm, w1_ref[:, slabs[c]], preferred_element_type=F32)
            a = jnp.maximum(a, 0.0)
            x = x + jnp.dot((a * a).astype(BF16), w2_ref[slabs[c - 1], :], preferred_element_type=F32)
            a = a_next
            yield
        a = jnp.maximum(a, 0.0)
        x = x + jnp.dot((a * a).astype(BF16), w2_ref[slabs[-1], :], preferred_element_type=F32)
        if final_norm:
            x = _rms(x, D_MODEL) * fg_ref[...]
        o_ref[rs, :] = x

    _interleave(*[part(slice(r, r + TOKEN_SUB)) for r in range(0, TOKEN_BLOCK, TOKEN_SUB)])


def _token_mix(x, ys, k, v, w_mix, gain_x, wq, wo, gain_mlp, w1, w2, final_gain, final_norm):
    b, s, _ = x.shape
    m = k.shape[1]
    row = lambda w: pl.BlockSpec((None, TOKEN_BLOCK, w), lambda i, j: (i, j, 0))
    mem = pl.BlockSpec((None, m, D_MODEL), lambda i, j: (i, 0, 0))
    const = lambda a: pl.BlockSpec(a.shape, lambda i, j: (0,) * a.ndim, pipeline_mode=pl.Buffered(1))
    weights = (w_mix, gain_x, wq, wo, gain_mlp, w1, w2, final_gain)
    return pl.pallas_call(
        functools.partial(_token_kernel, final_norm=final_norm),
        grid=(b, s // TOKEN_BLOCK),
        in_specs=[row(D_MODEL)] + [row(GROUP_WIDTH)] * 4 + [mem, mem] + [const(a) for a in weights],
        out_specs=row(D_MODEL),
        out_shape=jax.ShapeDtypeStruct((b, s, D_MODEL), F32),
        compiler_params=pltpu.CompilerParams(dimension_semantics=("parallel", "parallel"),
                                             vmem_limit_bytes=VMEM_LIMIT),
        name="token_mix",
    )(x, *ys, k, v, *weights)


def _pack_w_in(w):
    d = w.shape[0]
    zeros = lambda n: jnp.zeros((d, n), w.dtype)
    c0 = ZA_W + ZB_W
    cq = w[:, c0:c0 + MLA_Q_LORA]
    ckv = w[:, c0 + MLA_Q_LORA:c0 + MLA_Q_LORA + MLA_KV_LORA]
    kr = w[:, c0 + MLA_Q_LORA + MLA_KV_LORA:c0 + MLA_Q_LORA + MLA_KV_LORA + MLA_ROPE]
    d0 = c0 + MLA_Q_LORA + MLA_KV_LORA + MLA_ROPE
    rest = w[:, d0:]
    packed = jnp.concatenate([
        w[:, :c0],
        cq, zeros(ZC_KV - MLA_Q_LORA), ckv, zeros(MLA_NOPE), kr, zeros(LANES - MLA_NOPE - MLA_ROPE),
        rest, zeros(ZD_W - rest.shape[1]),
    ], axis=1)
    assert packed.shape[1] == Z_W
    return packed.astype(BF16)


def kernel(x, mem, positions, norm_mix, w_in, ret_gn, lru_conv_w, lru_conv_b, lru_wa, lru_ba, lru_wx, lru_bx, lru_lambda, mla_q_norm, mla_w_uq, mla_kv_norm, mla_w_ukv, gdn_conv_w, gdn_a_log, gdn_dt_bias, gdn_norm, w_out, norm_xattn, norm_mem, xa_wq, xa_wkv, xa_wo, norm_mlp, mlp_w1, mlp_w2, final_norm):
    b, s, d = x.shape
    depth = w_in.shape[0]
    pos_f = positions.astype(F32)
    vec = lambda v: v.reshape(1, -1)
    for l in range(depth):
        za, zb, zc, zd = _in_proj(x.reshape(b * s, d), vec(norm_mix[l]), _pack_w_in(w_in[l]))
        seq = lambda z: z.reshape(b, s, z.shape[-1])
        y_a, y_c = _ret_mla(seq(za), seq(zc), pos_f, ret_gn[l], mla_q_norm[l], mla_w_uq[l], mla_kv_norm[l],
                            mla_w_ukv[l])
        y_d, y_b = _gdn_lru(seq(zd), seq(zb), gdn_conv_w[l], gdn_a_log[l], gdn_dt_bias[l], gdn_norm[l],
                            lru_conv_w[l], lru_conv_b[l], lru_wa[l], lru_ba[l], lru_wx[l], lru_bx[l],
                            lru_lambda[l])
        k, v = _mem_kv(mem, vec(norm_mem[l]), xa_wkv[l].astype(BF16))
        x = _token_mix(x, (y_a, y_b, y_c, y_d), k, v, w_out[l].astype(BF16), vec(norm_xattn[l]),
                       xa_wq[l].astype(BF16), xa_wo[l].astype(BF16), vec(norm_mlp[l]),
                       mlp_w1[l].astype(BF16), mlp_w2[l].astype(BF16), vec(final_norm), l == depth - 1)
    return x
```

```python
import functools

import jax
import jax.numpy as jnp
from jax import lax
from jax.experimental import pallas as pl
from jax.experimental.pallas import tpu as pltpu

F32 = jnp.float32
BF16 = jnp.bfloat16

D_MODEL = 1024
CHUNK = 64
HEADS = 4
HEAD_DIM = 64
GROUP_WIDTH = HEADS * HEAD_DIM
EPS = 1e-6
LOG2_E = 1.4426950408889634
ROPE_THETA = 10000.0
LRU_C = 8.0
MLA_Q_LORA = 192
MLA_KV_LORA = 128
MLA_ROPE = 32
MLA_NOPE = 64
XA_HEADS = 4
XA_HEAD_DIM = D_MODEL // XA_HEADS
D_FF = 4 * D_MODEL
FF_SLAB = 1024
ATT_TILE = 512
ROPE_TILE = 256
GDN_ROWS = 256
SOLVE_BASE = 16
assert GDN_ROWS == GROUP_WIDTH and HEAD_DIM == CHUNK
ROW_TILE = 512
TOKEN_BLOCK = 512
TOKEN_SUB = 256
SUBLANES = 8
LANES = 128
VMEM_LIMIT = 56 * 1024 * 1024

ZA_W = 4 * GROUP_WIDTH
ZB_W = 2 * GROUP_WIDTH
ZC_W = 512
ZD_W = 1152
ZC_KV = 256
ZC_KR = 384
Z_W = ZA_W + ZB_W + ZC_W + ZD_W


def _dot(a, b):
    return jnp.dot(a.astype(BF16), b.astype(BF16), preferred_element_type=F32)


def _dot_nt(a, b):
    return lax.dot_general(a.astype(BF16), b.astype(BF16), (((1,), (1,)), ((), ())),
                           preferred_element_type=F32)


def _dot_tn(a, b):
    return lax.dot_general(a.astype(BF16), b.astype(BF16), (((0,), (0,)), ((), ())),
                           preferred_element_type=F32)


def _silu(x):
    return x * jax.nn.sigmoid(x)


def _softplus(x):
    return jnp.maximum(x, 0.0) + jnp.log1p(jnp.exp(-jnp.abs(x)))


def _expm1(y):
    e = jnp.exp(y)
    near = (y > -0.5) & (e != 1.0)
    safe_log = jnp.where(near, jnp.log(e), 1.0)
    return jnp.where(near, (e - 1.0) * y / safe_log, jnp.where(e == 1.0, y, e - 1.0))


def _rms(x, n):
    return x * lax.rsqrt(jnp.sum(x * x, axis=-1, keepdims=True) * (1.0 / n) + EPS)


def _layer_spec(stacked, layer, grid_rank, **kwargs):
    zeros = (0,) * (stacked.ndim - 1)
    index_map = (lambda i: (layer,) + zeros) if grid_rank == 1 else (lambda i, j: (layer,) + zeros)
    return pl.BlockSpec((None,) + stacked.shape[1:], index_map, **kwargs)


def _chunk_rows(n):
    return pl.ds(pl.multiple_of(n * CHUNK, CHUNK), CHUNK)


def _causal_conv_rows(z_ref, n, nrows, lo, hi, w):
    cur = z_ref[pl.ds(pl.multiple_of(n * nrows, nrows), nrows), lo:hi]
    prev_start = pl.multiple_of(jnp.maximum(n * nrows - SUBLANES, 0), SUBLANES)
    prev = z_ref[pl.ds(prev_start, SUBLANES), lo:hi]
    prev = jnp.where(n > 0, prev, 0.0)
    tile = jnp.concatenate([prev, cur], axis=0)
    acc = cur * w[3:4, :]
    for k in (1, 2, 3):
        acc = acc + pltpu.roll(tile, k, 0)[SUBLANES:, :] * w[3 - k:4 - k, :]
    return acc


def _lane_block_swap(x, half, first):
    return jnp.where(first, pltpu.roll(x, LANES - half, 1), pltpu.roll(x, half, 1))


def _in_proj_kernel(x_ref, g_ref, w_ref, za_ref, zb_ref, zc_ref, zd_ref):
    x = x_ref[...]
    h = (_rms(x, D_MODEL) * g_ref[...]).astype(BF16)
    lo = 0
    for ref in (za_ref, zb_ref, zc_ref, zd_ref):
        hi = lo + ref.shape[-1]
        ref[...] = jnp.dot(h, w_ref[:, lo:hi], preferred_element_type=F32)
        lo = hi


def _in_proj(x2d, gains, w_packed, layer):
    t = x2d.shape[0]
    row = lambda w: pl.BlockSpec((ROW_TILE, w), lambda i: (i, 0))
    return pl.pallas_call(
        _in_proj_kernel,
        grid=(t // ROW_TILE,),
        in_specs=[row(D_MODEL), _layer_spec(gains, layer, 1), _layer_spec(w_packed, layer, 1)],
        out_specs=[row(ZA_W), row(ZB_W), row(ZC_W), row(ZD_W)],
        out_shape=[jax.ShapeDtypeStruct((t, w), F32) for w in (ZA_W, ZB_W, ZC_W, ZD_W)],
        compiler_params=pltpu.CompilerParams(dimension_semantics=("parallel",),
                                             vmem_limit_bytes=VMEM_LIMIT),
        name="in_proj",
    )(x2d, gains, w_packed)


def _compact_positions(pos_f, n_freq):
    b, s = pos_f.shape
    nb = LANES // n_freq
    r = ROPE_TILE // nb
    p = pos_f.reshape(b, s // ROPE_TILE, nb, r).transpose(0, 1, 3, 2)
    return jnp.repeat(p, n_freq, axis=-1)


def _rope_expanders(n_freq, width, rope_lo, rope_hi):
    nb = LANES // n_freq
    src = jnp.arange(LANES)
    lane = jnp.arange(width)
    on_rope = (lane >= rope_lo) & (lane < rope_hi)
    hit = (src[:, None] % n_freq == lane[None, :] % n_freq) & on_rope[None, :]
    return jnp.stack([(hit & (src[:, None] // n_freq == r)) for r in range(nb)]).astype(BF16)


def _expand_rope_table(t_c, e_ref):
    return jnp.concatenate([_dot_exact(t_c, e_ref[r], 3) for r in range(e_ref.shape[0])], axis=0)


def _retention_stages(z_ref, posc_ref, invf_ref, e_ref, dmat_ref, xi_ref, zeta_ref, cd_ref, gn_ref,
                      o_ref, state_ref):
    gw = GROUP_WIDTH
    n_rows = ROPE_TILE
    lane = lax.broadcasted_iota(jnp.int32, (n_rows, LANES), 1)
    first = (lane % HEAD_DIM) < (HEAD_DIM // 2)
    lane_wide = lax.broadcasted_iota(jnp.int32, (n_rows, gw), 1)
    first_wide = (lane_wide % HEAD_DIM) < (HEAD_DIM // 2)
    ri = lax.broadcasted_iota(jnp.int32, (gw, gw), 0)
    ci = lax.broadcasted_iota(jnp.int32, (gw, gw), 1)
    same_head = (ri // HEAD_DIM) == (ci // HEAD_DIM)
    ones_bd = same_head.astype(BF16)
    state_ref[...] = jnp.zeros_like(state_ref)

    def rope(x, cos, sin_signed):
        parts = []
        for c in range(gw // LANES):
            sl = slice(c * LANES, (c + 1) * LANES)
            xs = _lane_block_swap(x[:, sl], HEAD_DIM // 2, first)
            parts.append(x[:, sl] * cos[:, sl] + xs * sin_signed[:, sl])
        return jnp.concatenate(parts, axis=1)

    def body(g):
        rows = pl.ds(pl.multiple_of(g * n_rows, n_rows), n_rows)
        ang = posc_ref[g] * invf_ref[...]
        cos = _expand_rope_table(jnp.cos(ang), e_ref)
        sin = _expand_rope_table(jnp.sin(ang), e_ref)
        yield
        sin_signed = jnp.where(first_wide, -sin, sin)
        q = (rope(z_ref[rows, 0:gw], cos, sin_signed) * (HEAD_DIM ** -0.5)).astype(BF16)
        k = rope(z_ref[rows, gw:2 * gw], cos, sin_signed)
        v = z_ref[rows, 2 * gw:3 * gw].astype(BF16)
        kz = (k * zeta_ref[...]).astype(BF16)
        k = k.astype(BF16)
        heads = [slice(h * HEAD_DIM, (h + 1) * HEAD_DIM) for h in range(HEADS)]
        blocks = [slice(c * CHUNK, (c + 1) * CHUNK) for c in range(n_rows // CHUNK)]
        scores = [_dot_nt(q[:, sl], k[:, sl]) for sl in heads]
        yield
        kvs = [_dot_tn(kz[blk], v[blk]) for blk in blocks]
        yield
        intra = [_dot(scores[h] * dmat_ref[h], v[:, sl])
                 for h, sl in enumerate(heads)]
        yield
        st = state_ref[...]
        cross = []
        for c, blk in enumerate(blocks):
            cross.append(_dot(q[blk], st))
            st = st * cd_ref[...] + jnp.where(same_head, kvs[c], 0.0)
        state_ref[...] = st
        yield
        o = jnp.concatenate(intra, axis=1) + jnp.concatenate(cross, axis=0) * xi_ref[...]
        mu = _dot_exact(o, ones_bd, 2) * (1.0 / HEAD_DIM)
        yield
        d = o - mu
        var = _dot_exact(d * d, ones_bd, 2) * (1.0 / HEAD_DIM)
        gate = z_ref[rows, 3 * gw:4 * gw]
        o_ref[rows, :] = d * lax.rsqrt(var + EPS) * gn_ref[...] * _silu(gate)

    return body


def _retention_consts(pos_f, ret_gn):
    h = HEADS
    half = HEAD_DIM // 2
    per = ROPE_TILE // CHUNK
    inv_freq = jnp.power(ROPE_THETA, -jnp.arange(half, dtype=F32) / half)
    invf = jnp.tile(inv_freq, LANES // half)[None, :]
    posc = _compact_positions(pos_f, half)
    expand = _rope_expanders(half, GROUP_WIDTH, 0, GROUP_WIDTH)
    log_gamma = jnp.log1p(-jnp.exp2(-5.0 - jnp.arange(h, dtype=F32)))
    idx = jnp.arange(CHUNK, dtype=F32)
    dmat = jnp.exp(jnp.abs(idx[:, None] - idx[None, :])[None] * log_gamma[:, None, None])
    dmat = jnp.stack([jnp.kron(jnp.eye(per, dtype=F32), dmat[i]) for i in range(h)])
    xi = jnp.repeat(jnp.exp((idx[:, None] + 1.0) * log_gamma), HEAD_DIM, axis=1)
    zeta = jnp.repeat(jnp.exp((CHUNK - 1.0 - idx)[:, None] * log_gamma), HEAD_DIM, axis=1)
    xi = jnp.tile(xi, (per, 1))
    zeta = jnp.tile(zeta, (per, 1))
    cd = jnp.repeat(jnp.exp(CHUNK * log_gamma), HEAD_DIM)[None, :]
    gn = ret_gn.reshape(1, GROUP_WIDTH)
    return posc, (invf, expand, dmat, xi, zeta, cd, gn)


def _interleave(*stage_generators):
    live = list(stage_generators)
    while live:
        for gen in list(live):
            try:
                next(gen)
            except StopIteration:
                live.remove(gen)


def _lru_stages(gi, z_ref, cw_ref, cb_ref, wa_ref, ba_ref, wx_ref, bx_ref, lam_ref, o_ref, h_ref):
    w = GROUP_WIDTH
    row = lax.broadcasted_iota(jnp.int32, (CHUNK, w), 0)
    for c in range(GDN_ROWS // CHUNK):
        n = gi * (GDN_ROWS // CHUNK) + c
        rows = _chunk_rows(n)
        xc = _causal_conv_rows(z_ref, n, CHUNK, 0, w, cw_ref[...]) + cb_ref[...]
        r = jax.nn.sigmoid(_dot(xc, wa_ref[...]) + ba_ref[...])
        i = jax.nn.sigmoid(_dot(xc, wx_ref[...]) + bx_ref[...])
        log_a = -LRU_C * r * _softplus(-lam_ref[...])
        a = jnp.exp(log_a)
        u = jnp.sqrt(-_expm1(2.0 * log_a)) * (i * xc)
        d = 1
        while d < CHUNK:
            a_sh = jnp.where(row >= d, pltpu.roll(a, d, 0), 1.0)
            u_sh = jnp.where(row >= d, pltpu.roll(u, d, 0), 0.0)
            u = a * u_sh + u
            a = a * a_sh
            d *= 2
        hs = u + a * h_ref[...]
        h_ref[...] = hs[CHUNK - 1:CHUNK, :]
        gate = z_ref[rows, w:2 * w]
        o_ref[rows, :] = hs * jax.nn.gelu(gate, approximate=True)
        yield


def _mla_parts(z_ref, posc_ref, invf_ref, e_ref, base_ref, qn_ref, wq_ref, kvn_ref, wk_ref, wvt_ref,
               o_ref, q_s, k_s, vt_s):
    s = z_ref.shape[0]
    nt = s // ATT_TILE
    exp2_coef = (MLA_NOPE + MLA_ROPE) ** -0.5 * LOG2_E
    lane = lax.broadcasted_iota(jnp.int32, (ROPE_TILE, LANES), 1)
    first = (lane >= MLA_NOPE) & (lane < MLA_NOPE + MLA_ROPE // 2)

    def rope(x, cos, sin_signed):
        return x * cos + _lane_block_swap(x, MLA_ROPE // 2, first) * sin_signed

    def project(t):
        rows = pl.ds(pl.multiple_of(t * ROPE_TILE, ROPE_TILE), ROPE_TILE)
        ang = posc_ref[t] * invf_ref[...]
        cos = _expand_rope_table(jnp.cos(ang), e_ref) + base_ref[...]
        sin = _expand_rope_table(jnp.sin(ang), e_ref)
        yield
        sin_signed = jnp.where(first, -sin, sin)
        cq = (_rms(z_ref[rows, 0:ZC_KV], MLA_Q_LORA) * qn_ref[...]).astype(BF16)
        ckv_f = _rms(z_ref[rows, ZC_KV:ZC_KR], MLA_KV_LORA) * kvn_ref[...]
        ckv = ckv_f.astype(BF16)
        k_rope = rope(z_ref[rows, ZC_KR:ZC_W], cos, sin_signed)
        vt = jnp.dot(wvt_ref[...], ckv_f.T.astype(BF16), preferred_element_type=F32)
        qs = [jnp.dot(cq, wq_ref[h], preferred_element_type=F32) for h in range(HEADS)]
        yield
        ks = [jnp.dot(ckv, wk_ref[h], preferred_element_type=F32) for h in range(HEADS)]
        vt_s[:, rows] = vt.astype(BF16)
        yield
        for h in range(HEADS):
            q_s[h, rows, :] = rope(qs[h], cos, sin_signed).astype(BF16)
            k_s[h, rows, :] = (ks[h] + k_rope).astype(BF16)

    key_id = lax.broadcasted_iota(jnp.int32, (ATT_TILE, ATT_TILE), 0) // CHUNK
    qry_id = lax.broadcasted_iota(jnp.int32, (ATT_TILE, ATT_TILE), 1) // CHUNK
    diag_visible = key_id <= qry_id

    def q_tile(i, carry):
        rows = pl.ds(pl.multiple_of(i * ATT_TILE, ATT_TILE), ATT_TILE)

        def kv_tile(j, state, on_diagonal):
            cols = pl.ds(pl.multiple_of(j * ATT_TILE, ATT_TILE), ATT_TILE)
            scores = [lax.dot_general(k_s[h, cols, :], q_s[h, rows, :], (((1,), (1,)), ((), ())),
                                      preferred_element_type=F32) for h in range(HEADS)]
            new_state = []
            for h in range(HEADS):
                m, l, acc = state[h]
                sc = scores[h]
                if on_diagonal:
                    sc = jnp.where(diag_visible, sc, -jnp.inf)
                m_new = jnp.maximum(m, jnp.max(sc, axis=0, keepdims=True))
                p = jnp.exp2((sc - m_new) * exp2_coef)
                alpha = jnp.exp2((m - m_new) * exp2_coef)
                l = alpha * l + jnp.sum(p, axis=0, keepdims=True)
                vt = vt_s[h * HEAD_DIM:(h + 1) * HEAD_DIM, cols]
                acc = alpha * acc + jnp.dot(vt, p.astype(BF16), preferred_element_type=F32)
                new_state.append((m_new, l, acc))
            return tuple(new_state)

        init = tuple((jnp.full((1, ATT_TILE), -jnp.inf, F32), jnp.zeros((1, ATT_TILE), F32),
                      jnp.zeros((HEAD_DIM, ATT_TILE), F32)) for _ in range(HEADS))
        state = lax.fori_loop(0, i, lambda j, st: kv_tile(j, st, False), init)
        state = kv_tile(i, state, True)
        o_t = jnp.concatenate([acc / l for _, l, acc in state], axis=0)
        o_ref[rows, :] = o_t.T
        return carry

    def attention():
        lax.fori_loop(0, nt, q_tile, 0)

    return project, attention


def _mla_consts(pos_f, q_norm, w_uq, kv_norm, w_ukv):
    h = HEADS
    half = MLA_ROPE // 2
    inv_freq = jnp.power(ROPE_THETA, -jnp.arange(half, dtype=F32) / half)
    invf = jnp.tile(inv_freq, LANES // half)[None, :]
    posc = _compact_positions(pos_f, half)
    expand = _rope_expanders(half, LANES, MLA_NOPE, MLA_NOPE + MLA_ROPE)
    lane = jnp.arange(LANES)
    base = ((lane < MLA_NOPE) | (lane >= MLA_NOPE + MLA_ROPE)).astype(F32)[None, :]
    qn = jnp.zeros((1, ZC_KV), F32).at[0, :MLA_Q_LORA].set(q_norm)
    kvn = kv_norm.reshape(1, MLA_KV_LORA)
    wq = w_uq.reshape(MLA_Q_LORA, h, MLA_NOPE + MLA_ROPE).transpose(1, 0, 2)
    wq = jnp.pad(wq, ((0, 0), (0, ZC_KV - MLA_Q_LORA), (0, LANES - MLA_NOPE - MLA_ROPE))).astype(BF16)
    wkv = w_ukv.reshape(MLA_KV_LORA, h, MLA_NOPE + HEAD_DIM)
    wk = jnp.pad(wkv[:, :, :MLA_NOPE].transpose(1, 0, 2),
                 ((0, 0), (0, 0), (0, LANES - MLA_NOPE))).astype(BF16)
    wvt = wkv[:, :, MLA_NOPE:].reshape(MLA_KV_LORA, h * HEAD_DIM).T.astype(BF16)
    return posc, (invf, expand, base, qn, wq, kvn, wk, wvt)


N_RET_CONSTS = 7
N_MLA_CONSTS = 8


def _ret_mla_kernel(*refs):
    za_ref, pa_ref = refs[0:2]
    ret_consts = refs[2:2 + N_RET_CONSTS]
    zc_ref, pc_ref = refs[2 + N_RET_CONSTS:4 + N_RET_CONSTS]
    mla_consts = refs[4 + N_RET_CONSTS:4 + N_RET_CONSTS + N_MLA_CONSTS]
    oa_ref, oc_ref, state_ref, q_s, k_s, vt_s = refs[4 + N_RET_CONSTS + N_MLA_CONSTS:]
    retention = _retention_stages(za_ref, pa_ref, *ret_consts, oa_ref, state_ref)
    project, attention = _mla_parts(zc_ref, pc_ref, *mla_consts, oc_ref, q_s, k_s, vt_s)

    def group(g, carry):
        _interleave(retention(g), project(g))
        return carry

    lax.fori_loop(0, za_ref.shape[0] // ROPE_TILE, group, 0)
    attention()


def _ret_mla(za, zc, pos_f, ret_gn, q_norm, w_uq, kv_norm, w_ukv):
    b, s, _ = za.shape
    pos_a, ret_consts = _retention_consts(pos_f, ret_gn)
    pos_c, mla_consts = _mla_consts(pos_f, q_norm, w_uq, kv_norm, w_ukv)
    assert len(ret_consts) == N_RET_CONSTS and len(mla_consts) == N_MLA_CONSTS
    full = lambda a: pl.BlockSpec(a.shape, lambda i: (0,) * a.ndim)
    seq = lambda w: pl.BlockSpec((None, s, w), lambda i: (i, 0, 0))
    tiles = lambda p: pl.BlockSpec((None,) + p.shape[1:], lambda i: (i, 0, 0, 0))
    return pl.pallas_call(
        _ret_mla_kernel,
        grid=(b,),
        in_specs=[seq(ZA_W), tiles(pos_a)] + [full(a) for a in ret_consts]
                 + [seq(ZC_W), tiles(pos_c)] + [full(a) for a in mla_consts],
        out_specs=[seq(GROUP_WIDTH), seq(GROUP_WIDTH)],
        out_shape=[jax.ShapeDtypeStruct((b, s, GROUP_WIDTH), F32)] * 2,
        scratch_shapes=[pltpu.VMEM((GROUP_WIDTH, GROUP_WIDTH), F32),
                        pltpu.VMEM((HEADS, s, LANES), BF16), pltpu.VMEM((HEADS, s, LANES), BF16),
                        pltpu.VMEM((GROUP_WIDTH, s), BF16)],
        compiler_params=pltpu.CompilerParams(dimension_semantics=("parallel",),
                                             vmem_limit_bytes=VMEM_LIMIT),
        name="ret_mla",
    )(za, pos_a, *ret_consts, zc, pos_c, *mla_consts)


def _bf16_parts(x, parts):
    out = []
    for _ in range(parts):
        p = x.astype(BF16)
        out.append(p)
        x = x - p.astype(F32)
    return out


def _dot_exact(a, b, parts=None):
    f32_is_lhs = isinstance(a, list) or a.dtype == F32
    split = a if f32_is_lhs else b
    if not isinstance(split, list):
        split = _bf16_parts(split, parts)
    terms = [jnp.dot(p, b, preferred_element_type=F32) if f32_is_lhs
             else jnp.dot(a, p, preferred_element_type=F32) for p in split]
    acc = terms[-1]
    for t in terms[-2::-1]:
        acc = acc + t
    return acc


def _gdn_lru_kernel(z_ref, cw_ref, alog_ref, dtb_ref, ng_ref,
                    zb_ref, lcw_ref, lcb_ref, lwa_ref, lba_ref, lwx_ref, lbx_ref, llam_ref,
                    o_ref, ob_ref, u_s, wq_s, kd_s, attn_s, gl_s, state_ref, h_ref):
    s = z_ref.shape[0]
    gw = GROUP_WIDTH
    g_rows = GDN_ROWS
    per = g_rows // CHUNK
    ri = lax.broadcasted_iota(jnp.int32, (g_rows, g_rows), 0)
    ci = lax.broadcasted_iota(jnp.int32, (g_rows, g_rows), 1)
    block_mask = lambda width: (ri // width) == (ci // width)
    eye = ri == ci
    same_block = block_mask(CHUNK)
    causal = same_block & (ri >= ci)
    strict = same_block & (ri > ci)
    tri = causal.astype(BF16)
    ones_bd = same_block.astype(BF16)
    er = lax.broadcasted_iota(jnp.int32, (LANES, gw), 0)
    ec = lax.broadcasted_iota(jnp.int32, (LANES, gw), 1) // HEAD_DIM
    expand_beta = (er == ec).astype(BF16)
    expand_alpha = (er == ec + HEADS).astype(BF16)

    def precompute(gi):
        rows = pl.ds(pl.multiple_of(gi * g_rows, g_rows), g_rows)
        qkv = _silu(_causal_conv_rows(z_ref, gi, g_rows, 0, 3 * gw, cw_ref[...]))
        q = qkv[:, 0:gw]
        k = qkv[:, gw:2 * gw]
        v = qkv[:, 2 * gw:3 * gw]
        small = z_ref[rows, 4 * gw:4 * gw + LANES]
        beta_t = jax.nn.sigmoid(small)
        g_t = -jnp.exp(alog_ref[...]) * _softplus(small + dtb_ref[...])
        qn = q * lax.rsqrt(_dot_exact(q * q, ones_bd, 2) + EPS) * (HEAD_DIM ** -0.5)
        kn = k * lax.rsqrt(_dot_exact(k * k, ones_bd, 2) + EPS)
        beta_w = _dot_exact(beta_t, expand_beta, 3)
        g_t_parts = _bf16_parts(g_t, 3)
        g_w = _dot_exact(g_t_parts, expand_alpha)
        gc_t = _dot_exact(tri, g_t_parts)
        yield
        g_w_parts = _bf16_parts(g_w, 3)
        gc_w = _dot_exact(tri, g_w_parts)
        gl_w = _dot_exact(ones_bd, g_w_parts)
        gc_tt = gc_t.T
        kb_w = kn * beta_w
        vb_w = v * beta_w
        heads = [slice(h * HEAD_DIM, (h + 1) * HEAD_DIM) for h in range(HEADS)]
        kh = [kn[:, sl].astype(BF16) for sl in heads]
        grams = [_dot_nt(kb_w[:, sl], kh[h]) for h, sl in enumerate(heads)]
        yield
        qks = [_dot_nt(qn[:, sl], kh[h]) for h, sl in enumerate(heads)]
        egc_w = jnp.exp(gc_w)
        kbe_w = kb_w * egc_w
        yield
        lowers, rhs = [], []
        for h, sl in enumerate(heads):
            gcol = jnp.broadcast_to(gc_t[:, HEADS + h:HEADS + h + 1], (g_rows, g_rows))
            grow = jnp.broadcast_to(gc_tt[HEADS + h:HEADS + h + 1, :], (g_rows, g_rows))
            decay = jnp.exp(jnp.where(causal, gcol - grow, -jnp.inf))
            lowers.append(jnp.where(strict, grams[h] * decay, 0.0))
            attn_s[h, rows, :] = (qks[h] * decay).astype(BF16)
            rhs.append(jnp.concatenate([vb_w[:, sl], kbe_w[:, sl]], axis=1))
        bf = lambda xs: [x.astype(BF16) for x in xs]
        plus_eye = lambda x: jnp.where(eye, x + 1.0, x)
        inner = block_mask(SOLVE_BASE)
        d = [jnp.where(inner, low, 0.0) for low in lowers]
        factors = [[jnp.where(eye, 1.0, -x) for x in d]]
        power = bf(d)
        width = 2
        while width < SOLVE_BASE:
            power_f = [_dot(p, p) for p in power]
            yield
            factors.append([plus_eye(x) for x in power_f])
            power = bf(power_f)
            width *= 2
        while len(factors) > 1:
            factors = [[_dot(x, y) for x, y in zip(factors[i], factors[i + 1])] if i + 1 < len(factors)
                       else factors[i] for i in range(0, len(factors), 2)]
            yield
        t = factors[0]
        width = SOLVE_BASE
        while width < CHUNK:
            outer = block_mask(2 * width)
            e = [jnp.where(outer & ~inner, low, 0.0).astype(BF16) for low in lowers]
            t_b = bf(t)
            et = [_dot(x, y) for x, y in zip(e, t_b)]
            yield
            t = [x - _dot(y, z) for x, y, z in zip(t, t_b, et)]
            yield
            inner = outer
            width *= 2
        sols = [_dot(x, y) for x, y in zip(t, rhs)]
        yield
        u_s[rows, :] = jnp.concatenate([sol[:, :HEAD_DIM] for sol in sols], axis=1)
        w_all = jnp.concatenate([sol[:, HEAD_DIM:] for sol in sols], axis=1).astype(BF16)
        qd_w = (qn * egc_w).astype(BF16)
        kd_s[rows, :] = (kn * jnp.exp(gl_w - gc_w)).astype(BF16)
        for c in range(per):
            blk = slice(c * CHUNK, (c + 1) * CHUNK)
            wq_s[gi * per + c, 0:CHUNK, :] = w_all[blk]
            wq_s[gi * per + c, CHUNK:2 * CHUNK, :] = qd_w[blk]
            gl_s[gi * per + c] = gl_w[c * CHUNK:c * CHUNK + 1, :]

    def recur(gi):
        st = state_ref[...]
        outs = []
        for c in range(per):
            n = gi * per + c
            rows = _chunk_rows(n)
            r1 = _dot(wq_s[n], st)
            yield
            v_new = (u_s[rows, :] - r1[0:CHUNK]).astype(BF16)
            ds = lax.dot_general(kd_s[rows, :], v_new, (((0,), (0,)), ((), ())),
                                 preferred_element_type=F32)
            parts = []
            for h in range(HEADS):
                sl = slice(h * HEAD_DIM, (h + 1) * HEAD_DIM)
                a = attn_s[h, rows, c * CHUNK:(c + 1) * CHUNK]
                parts.append(jnp.dot(a, v_new[:, sl], preferred_element_type=F32))
            yield
            outs.append(r1[CHUNK:2 * CHUNK] + jnp.concatenate(parts, axis=1))
            st = st * jnp.exp(gl_s[n]) + jnp.where(same_block, ds, 0.0)
        state_ref[...] = st
        rows = pl.ds(pl.multiple_of(gi * g_rows, g_rows), g_rows)
        o = jnp.concatenate(outs, axis=0)
        ms = _dot_exact(o * o, ones_bd, 2) * (1.0 / HEAD_DIM)
        gate = z_ref[rows, 3 * gw:4 * gw]
        o_ref[rows, :] = o * lax.rsqrt(ms + EPS) * ng_ref[...] * _silu(gate)

    def lru(gi):
        return _lru_stages(gi, zb_ref, lcw_ref, lcb_ref, lwa_ref, lba_ref, lwx_ref, lbx_ref, llam_ref,
                           ob_ref, h_ref)

    state_ref[...] = jnp.zeros_like(state_ref)
    h_ref[...] = jnp.zeros_like(h_ref)
    n_groups = s // g_rows
    _interleave(precompute(0), lru(0))

    def steady(gi, carry):
        _interleave(recur(gi - 1), precompute(gi), lru(gi))
        return carry

    lax.fori_loop(1, n_groups, steady, 0)
    _interleave(recur(n_groups - 1))


def _gdn_lru(zd, zb, conv_w, a_log, dt_bias, norm_g, lru_conv_w, lru_conv_b, wa, ba, wx, bx, lam):
    b, s, _ = zd.shape
    lane_vec = lambda p: jnp.zeros((1, LANES), F32).at[0, HEADS:2 * HEADS].set(p)
    ng = jnp.tile(norm_g, HEADS)[None, :]
    bd = lambda m: jax.scipy.linalg.block_diag(*[m[i] for i in range(m.shape[0])]).astype(BF16)
    vec = lambda v: v.reshape(1, GROUP_WIDTH)
    gdn_args = (conv_w, lane_vec(a_log), lane_vec(dt_bias), ng)
    lru_args = (lru_conv_w, vec(lru_conv_b), bd(wa), vec(ba), bd(wx), vec(bx), vec(lam))
    full = lambda a: pl.BlockSpec(a.shape, lambda i: (0,) * a.ndim)
    seq = lambda w: pl.BlockSpec((None, s, w), lambda i: (i, 0, 0))
    return pl.pallas_call(
        _gdn_lru_kernel,
        grid=(b,),
        in_specs=[seq(ZD_W)] + [full(a) for a in gdn_args] + [seq(ZB_W)] + [full(a) for a in lru_args],
        out_specs=[seq(GROUP_WIDTH), seq(GROUP_WIDTH)],
        out_shape=[jax.ShapeDtypeStruct((b, s, GROUP_WIDTH), F32)] * 2,
        scratch_shapes=[pltpu.VMEM((s, GROUP_WIDTH), F32),
                        pltpu.VMEM((s // CHUNK, 2 * CHUNK, GROUP_WIDTH), BF16),
                        pltpu.VMEM((s, GROUP_WIDTH), BF16),
                        pltpu.VMEM((HEADS, s, GDN_ROWS), BF16),
                        pltpu.VMEM((s // CHUNK, 1, GROUP_WIDTH), F32),
                        pltpu.VMEM((GROUP_WIDTH, GROUP_WIDTH), F32),
                        pltpu.VMEM((1, GROUP_WIDTH), F32)],
        compiler_params=pltpu.CompilerParams(dimension_semantics=("parallel",),
                                             vmem_limit_bytes=VMEM_LIMIT),
        name="gdn_lru",
    )(zd, *gdn_args, zb, *lru_args)


def _mem_kv_kernel(mem_ref, g_ref, wkv_ref, k_ref, v_ref):
    m = (_rms(mem_ref[...], D_MODEL) * g_ref[...]).astype(BF16)
    k_ref[...] = jnp.dot(m, wkv_ref[:, :D_MODEL], preferred_element_type=F32).astype(BF16)
    v_ref[...] = jnp.dot(m, wkv_ref[:, D_MODEL:], preferred_element_type=F32).astype(BF16)


def _mem_kv(mem, gains, wkv, layer):
    b, m, _ = mem.shape
    blk = pl.BlockSpec((None, m, D_MODEL), lambda i: (i, 0, 0))
    return pl.pallas_call(
        _mem_kv_kernel,
        grid=(b,),
        in_specs=[blk, _layer_spec(gains, layer, 1), _layer_spec(wkv, layer, 1)],
        out_specs=[blk, blk],
        out_shape=[jax.ShapeDtypeStruct((b, m, D_MODEL), BF16)] * 2,
        compiler_params=pltpu.CompilerParams(dimension_semantics=("parallel",),
                                             vmem_limit_bytes=VMEM_LIMIT),
        name="mem_kv",
    )(mem, gains, wkv)


def _token_kernel(x_ref, ya_ref, yb_ref, yc_ref, yd_ref, k_ref, v_ref, wmix_ref, gx_ref, wq_ref, wo_ref,
                  gm_ref, w1_ref, w2_ref, fg_ref, o_ref, *, final_norm):
    def part(rs):
        x = x_ref[rs, :]
        for gi, y_ref in enumerate((ya_ref, yb_ref, yc_ref, yd_ref)):
            x = x + jnp.dot(y_ref[rs, :].astype(BF16), wmix_ref[gi * GROUP_WIDTH:(gi + 1) * GROUP_WIDTH, :],
                            preferred_element_type=F32)
        yield
        hq = (_rms(x, D_MODEL) * gx_ref[...]).astype(BF16)
        q = jnp.dot(hq, wq_ref[...], preferred_element_type=F32).astype(BF16)
        yield
        heads = [slice(h * XA_HEAD_DIM, (h + 1) * XA_HEAD_DIM) for h in range(XA_HEADS)]
        scores = [lax.dot_general(q[:, sl], k_ref[:, sl], (((1,), (1,)), ((), ())),
                                  preferred_element_type=F32) for sl in heads]
        yield
        outs = []
        for h, sl in enumerate(heads):
            sc = scores[h] * (XA_HEAD_DIM ** -0.5)
            e = jnp.exp(sc - jnp.max(sc, axis=-1, keepdims=True))
            p = e / jnp.sum(e, axis=-1, keepdims=True)
            outs.append(jnp.dot(p.astype(BF16), v_ref[:, sl], preferred_element_type=F32).astype(BF16))
        yield
        x = x + jnp.dot(jnp.concatenate(outs, axis=1), wo_ref[...], preferred_element_type=F32)
        yield
        hm = (_rms(x, D_MODEL) * gm_ref[...]).astype(BF16)
        slabs = [slice(c * FF_SLAB, (c + 1) * FF_SLAB) for c in range(D_FF // FF_SLAB)]
        a = jnp.dot(hm, w1_ref[:, slabs[0]], preferred_element_type=F32)
        yield
        for c in range(1, len(slabs)):
            a_next = jnp.dot(hm, w1_ref[:, slabs[c]], preferred_element_type=F32)
            a = jnp.maximum(a, 0.0)
            x = x + jnp.dot((a * a).astype(BF16), w2_ref[slabs[c - 1], :], preferred_element_type=F32)
            a = a_next
            yield
        a = jnp.maximum(a, 0.0)
        x = x + jnp.dot((a * a).astype(BF16), w2_ref[slabs[-1], :], preferred_element_type=F32)
        if final_norm:
            x = _rms(x, D_MODEL) * fg_ref[...]
        o_ref[rs, :] = x

    _interleave(*[part(slice(r, r + TOKEN_SUB)) for r in range(0, TOKEN_BLOCK, TOKEN_SUB)])


def _token_mix(x, ys, k, v, weights, final_gain, layer, final_norm):
    b, s, _ = x.shape
    m = k.shape[1]
    row = lambda w: pl.BlockSpec((None, TOKEN_BLOCK, w), lambda i, j: (i, j, 0))
    mem = pl.BlockSpec((None, m, D_MODEL), lambda i, j: (i, 0, 0))
    const = lambda a: _layer_spec(a, layer, 2, pipeline_mode=pl.Buffered(1))
    final = pl.BlockSpec(final_gain.shape, lambda i, j: (0, 0), pipeline_mode=pl.Buffered(1))
    return pl.pallas_call(
        functools.partial(_token_kernel, final_norm=final_norm),
        grid=(b, s // TOKEN_BLOCK),
        in_specs=[row(D_MODEL)] + [row(GROUP_WIDTH)] * 4 + [mem, mem] + [const(a) for a in weights] + [final],
        out_specs=row(D_MODEL),
        out_shape=jax.ShapeDtypeStruct((b, s, D_MODEL), F32),
        compiler_params=pltpu.CompilerParams(dimension_semantics=("parallel", "parallel"),
                                             vmem_limit_bytes=VMEM_LIMIT),
        name="token_mix",
    )(x, *ys, k, v, *weights, final_gain)


def _pack_w_in(w):
    zeros = lambda n: jnp.zeros(w.shape[:-1] + (n,), w.dtype)
    c0 = ZA_W + ZB_W
    cq = w[..., c0:c0 + MLA_Q_LORA]
    ckv = w[..., c0 + MLA_Q_LORA:c0 + MLA_Q_LORA + MLA_KV_LORA]
    kr = w[..., c0 + MLA_Q_LORA + MLA_KV_LORA:c0 + MLA_Q_LORA + MLA_KV_LORA + MLA_ROPE]
    d0 = c0 + MLA_Q_LORA + MLA_KV_LORA + MLA_ROPE
    rest = w[..., d0:]
    packed = jnp.concatenate([
        w[..., :c0],
        cq, zeros(ZC_KV - MLA_Q_LORA), ckv, zeros(MLA_NOPE), kr, zeros(LANES - MLA_NOPE - MLA_ROPE),
        rest, zeros(ZD_W - rest.shape[-1]),
    ], axis=-1)
    assert packed.shape[-1] == Z_W
    return packed.astype(BF16)


def kernel(x, mem, positions, norm_mix, w_in, ret_gn, lru_conv_w, lru_conv_b, lru_wa, lru_ba, lru_wx, lru_bx, lru_lambda, mla_q_norm, mla_w_uq, mla_kv_norm, mla_w_ukv, gdn_conv_w, gdn_a_log, gdn_dt_bias, gdn_norm, w_out, norm_xattn, norm_mem, xa_wq, xa_wkv, xa_wo, norm_mlp, mlp_w1, mlp_w2, final_norm):
    b, s, d = x.shape
    depth = w_in.shape[0]
    pos_f = positions.astype(F32)
    rows = lambda g: g[:, None, :]
    w_in_packed = _pack_w_in(w_in)
    wkv = xa_wkv.astype(BF16)
    token_weights = (w_out.astype(BF16), rows(norm_xattn), xa_wq.astype(BF16), xa_wo.astype(BF16),
                     rows(norm_mlp), mlp_w1.astype(BF16), mlp_w2.astype(BF16))
    for l in range(depth):
        za, zb, zc, zd = _in_proj(x.reshape(b * s, d), rows(norm_mix), w_in_packed, l)
        seq = lambda z: z.reshape(b, s, z.shape[-1])
        y_a, y_c = _ret_mla(seq(za), seq(zc), pos_f, ret_gn[l], mla_q_norm[l], mla_w_uq[l], mla_kv_norm[l],
                            mla_w_ukv[l])
        y_d, y_b = _gdn_lru(seq(zd), seq(zb), gdn_conv_w[l], gdn_a_log[l], gdn_dt_bias[l], gdn_norm[l],
                            lru_conv_w[l], lru_conv_b[l], lru_wa[l], lru_ba[l], lru_wx[l], lru_bx[l],
                            lru_lambda[l])
        k, v = _mem_kv(mem, rows(norm_mem), wkv, l)
        x = _token_mix(x, (y_a, y_b, y_c, y_d), k, v, token_weights, final_norm[None, :], l, l == depth - 1)
    return x
```

```python
import functools

import jax
import jax.numpy as jnp
from jax import lax
from jax.experimental import pallas as pl
from jax.experimental.pallas import tpu as pltpu

F32 = jnp.float32
BF16 = jnp.bfloat16

D_MODEL = 1024
CHUNK = 64
HEADS = 4
HEAD_DIM = 64
GROUP_WIDTH = HEADS * HEAD_DIM
EPS = 1e-6
LOG2_E = 1.4426950408889634
ROPE_THETA = 10000.0
LRU_C = 8.0
MLA_Q_LORA = 192
MLA_KV_LORA = 128
MLA_ROPE = 32
MLA_NOPE = 64
XA_HEADS = 4
XA_HEAD_DIM = D_MODEL // XA_HEADS
D_FF = 4 * D_MODEL
FF_SLAB = 1024
ATT_TILE = 512
ROPE_TILE = 256
GDN_ROWS = 256
SOLVE_BASE = 16
assert GDN_ROWS == GROUP_WIDTH and HEAD_DIM == CHUNK
ROW_TILE = 512
TOKEN_BLOCK = 512
TOKEN_SUB = 256
SUBLANES = 8
LANES = 128
VMEM_LIMIT = 56 * 1024 * 1024

ZA_W = 4 * GROUP_WIDTH
ZB_W = 2 * GROUP_WIDTH
ZC_W = 512
ZD_W = 1152
ZC_KV = 256
ZC_KR = 384
Z_W = ZA_W + ZB_W + ZC_W + ZD_W


def _dot(a, b):
    return jnp.dot(a.astype(BF16), b.astype(BF16), preferred_element_type=F32)


def _dot_nt(a, b):
    return lax.dot_general(a.astype(BF16), b.astype(BF16), (((1,), (1,)), ((), ())),
                           preferred_element_type=F32)


def _dot_tn(a, b):
    return lax.dot_general(a.astype(BF16), b.astype(BF16), (((0,), (0,)), ((), ())),
                           preferred_element_type=F32)


def _silu(x):
    return x * jax.nn.sigmoid(x)


def _softplus(x):
    return jnp.maximum(x, 0.0) + jnp.log1p(jnp.exp(-jnp.abs(x)))


def _expm1(y):
    e = jnp.exp(y)
    near = (y > -0.5) & (e != 1.0)
    safe_log = jnp.where(near, jnp.log(e), 1.0)
    return jnp.where(near, (e - 1.0) * y / safe_log, jnp.where(e == 1.0, y, e - 1.0))


def _rms(x, n):
    return x * lax.rsqrt(jnp.sum(x * x, axis=-1, keepdims=True) * (1.0 / n) + EPS)


def _layer_spec(stacked, layer, grid_rank, **kwargs):
    zeros = (0,) * (stacked.ndim - 1)
    index_map = (lambda i: (layer,) + zeros) if grid_rank == 1 else (lambda i, j: (layer,) + zeros)
    return pl.BlockSpec((None,) + stacked.shape[1:], index_map, **kwargs)


def _chunk_rows(n):
    return pl.ds(pl.multiple_of(n * CHUNK, CHUNK), CHUNK)


def _causal_conv_rows(z_ref, n, nrows, lo, hi, w):
    cur = z_ref[pl.ds(pl.multiple_of(n * nrows, nrows), nrows), lo:hi]
    prev_start = pl.multiple_of(jnp.maximum(n * nrows - SUBLANES, 0), SUBLANES)
    prev = z_ref[pl.ds(prev_start, SUBLANES), lo:hi]
    prev = jnp.where(n > 0, prev, 0.0)
    tile = jnp.concatenate([prev, cur], axis=0)
    acc = cur * w[3:4, :]
    for k in (1, 2, 3):
        acc = acc + pltpu.roll(tile, k, 0)[SUBLANES:, :] * w[3 - k:4 - k, :]
    return acc


def _lane_block_swap(x, half, first):
    return jnp.where(first, pltpu.roll(x, LANES - half, 1), pltpu.roll(x, half, 1))


def _in_proj_kernel(x_ref, g_ref, w_ref, za_ref, zb_ref, zc_ref, zd_ref):
    x = x_ref[...]
    h = (_rms(x, D_MODEL) * g_ref[...]).astype(BF16)
    lo = 0
    for ref in (za_ref, zb_ref, zc_ref, zd_ref):
        hi = lo + ref.shape[-1]
        ref[...] = jnp.dot(h, w_ref[:, lo:hi], preferred_element_type=F32)
        lo = hi


def _in_proj(x2d, gains, w_packed, layer):
    t = x2d.shape[0]
    row = lambda w: pl.BlockSpec((ROW_TILE, w), lambda i: (i, 0))
    return pl.pallas_call(
        _in_proj_kernel,
        grid=(t // ROW_TILE,),
        in_specs=[row(D_MODEL), _layer_spec(gains, layer, 1), _layer_spec(w_packed, layer, 1)],
        out_specs=[row(ZA_W), row(ZB_W), row(ZC_W), row(ZD_W)],
        out_shape=[jax.ShapeDtypeStruct((t, w), F32) for w in (ZA_W, ZB_W, ZC_W, ZD_W)],
        compiler_params=pltpu.CompilerParams(dimension_semantics=("parallel",),
                                             vmem_limit_bytes=VMEM_LIMIT),
        name="in_proj",
    )(x2d, gains, w_packed)


def _compact_positions(pos_f, n_freq):
    b, s = pos_f.shape
    nb = LANES // n_freq
    r = ROPE_TILE // nb
    p = pos_f.reshape(b, s // ROPE_TILE, nb, r).transpose(0, 1, 3, 2)
    return jnp.repeat(p, n_freq, axis=-1)


def _rope_expanders(n_freq, width, rope_lo, rope_hi):
    nb = LANES // n_freq
    src = jnp.arange(LANES)
    lane = jnp.arange(width)
    on_rope = (lane >= rope_lo) & (lane < rope_hi)
    hit = (src[:, None] % n_freq == lane[None, :] % n_freq) & on_rope[None, :]
    return jnp.stack([(hit & (src[:, None] // n_freq == r)) for r in range(nb)]).astype(BF16)


def _expand_rope_table(t_c, e_ref):
    return jnp.concatenate([_dot_exact(t_c, e_ref[r], 3) for r in range(e_ref.shape[0])], axis=0)


def _retention_stages(z_ref, posc_ref, invf_ref, e_ref, dmat_ref, xi_ref, zeta_ref, cd_ref, gn_ref,
                      o_ref, state_ref):
    gw = GROUP_WIDTH
    n_rows = ROPE_TILE
    lane = lax.broadcasted_iota(jnp.int32, (n_rows, LANES), 1)
    first = (lane % HEAD_DIM) < (HEAD_DIM // 2)
    lane_wide = lax.broadcasted_iota(jnp.int32, (n_rows, gw), 1)
    first_wide = (lane_wide % HEAD_DIM) < (HEAD_DIM // 2)
    ri = lax.broadcasted_iota(jnp.int32, (gw, gw), 0)
    ci = lax.broadcasted_iota(jnp.int32, (gw, gw), 1)
    same_head = (ri // HEAD_DIM) == (ci // HEAD_DIM)
    ones_bd = same_head.astype(BF16)
    state_ref[...] = jnp.zeros_like(state_ref)

    def rope(x, cos, sin_signed):
        parts = []
        for c in range(gw // LANES):
            sl = slice(c * LANES, (c + 1) * LANES)
            xs = _lane_block_swap(x[:, sl], HEAD_DIM // 2, first)
            parts.append(x[:, sl] * cos[:, sl] + xs * sin_signed[:, sl])
        return jnp.concatenate(parts, axis=1)

    def body(g):
        rows = pl.ds(pl.multiple_of(g * n_rows, n_rows), n_rows)
        ang = posc_ref[g] * invf_ref[...]
        cos = _expand_rope_table(jnp.cos(ang), e_ref)
        sin = _expand_rope_table(jnp.sin(ang), e_ref)
        yield
        sin_signed = jnp.where(first_wide, -sin, sin)
        q = (rope(z_ref[rows, 0:gw], cos, sin_signed) * (HEAD_DIM ** -0.5)).astype(BF16)
        k = rope(z_ref[rows, gw:2 * gw], cos, sin_signed)
        v = z_ref[rows, 2 * gw:3 * gw].astype(BF16)
        kz = (k * zeta_ref[...]).astype(BF16)
        k = k.astype(BF16)
        heads = [slice(h * HEAD_DIM, (h + 1) * HEAD_DIM) for h in range(HEADS)]
        blocks = [slice(c * CHUNK, (c + 1) * CHUNK) for c in range(n_rows // CHUNK)]
        scores = [_dot_nt(q[:, sl], k[:, sl]) for sl in heads]
        yield
        kvs = [_dot_tn(kz[blk], v[blk]) for blk in blocks]
        yield
        intra = [_dot(scores[h] * dmat_ref[h], v[:, sl])
                 for h, sl in enumerate(heads)]
        yield
        st = state_ref[...]
        cross = []
        for c, blk in enumerate(blocks):
            cross.append(_dot(q[blk], st))
            st = st * cd_ref[...] + jnp.where(same_head, kvs[c], 0.0)
        state_ref[...] = st
        yield
        o = jnp.concatenate(intra, axis=1) + jnp.concatenate(cross, axis=0) * xi_ref[...]
        mu = _dot_exact(o, ones_bd, 2) * (1.0 / HEAD_DIM)
        yield
        d = o - mu
        var = _dot_exact(d * d, ones_bd, 2) * (1.0 / HEAD_DIM)
        gate = z_ref[rows, 3 * gw:4 * gw]
        o_ref[rows, :] = d * lax.rsqrt(var + EPS) * gn_ref[...] * _silu(gate)

    return body


def _retention_consts(pos_f, ret_gn):
    h = HEADS
    half = HEAD_DIM // 2
    per = ROPE_TILE // CHUNK
    inv_freq = jnp.power(ROPE_THETA, -jnp.arange(half, dtype=F32) / half)
    invf = jnp.tile(inv_freq, LANES // half)[None, :]
    posc = _compact_positions(pos_f, half)
    expand = _rope_expanders(half, GROUP_WIDTH, 0, GROUP_WIDTH)
    log_gamma = jnp.log1p(-jnp.exp2(-5.0 - jnp.arange(h, dtype=F32)))
    idx = jnp.arange(CHUNK, dtype=F32)
    dmat = jnp.exp(jnp.abs(idx[:, None] - idx[None, :])[None] * log_gamma[:, None, None])
    dmat = jnp.stack([jnp.kron(jnp.eye(per, dtype=F32), dmat[i]) for i in range(h)])
    xi = jnp.repeat(jnp.exp((idx[:, None] + 1.0) * log_gamma), HEAD_DIM, axis=1)
    zeta = jnp.repeat(jnp.exp((CHUNK - 1.0 - idx)[:, None] * log_gamma), HEAD_DIM, axis=1)
    xi = jnp.tile(xi, (per, 1))
    zeta = jnp.tile(zeta, (per, 1))
    cd = jnp.repeat(jnp.exp(CHUNK * log_gamma), HEAD_DIM)[None, :]
    gn = ret_gn.reshape(1, GROUP_WIDTH)
    return posc, (invf, expand, dmat, xi, zeta, cd, gn)


def _interleave(*stage_generators):
    live = list(stage_generators)
    while live:
        for gen in list(live):
            try:
                next(gen)
            except StopIteration:
                live.remove(gen)


def _lru_stages(gi, z_ref, cw_ref, cb_ref, wa_ref, ba_ref, wx_ref, bx_ref, lam_ref, o_ref, h_ref):
    w = GROUP_WIDTH
    row = lax.broadcasted_iota(jnp.int32, (CHUNK, w), 0)
    for c in range(GDN_ROWS // CHUNK):
        n = gi * (GDN_ROWS // CHUNK) + c
        rows = _chunk_rows(n)
        xc = _causal_conv_rows(z_ref, n, CHUNK, 0, w, cw_ref[...]) + cb_ref[...]
        r = jax.nn.sigmoid(_dot(xc, wa_ref[...]) + ba_ref[...])
        i = jax.nn.sigmoid(_dot(xc, wx_ref[...]) + bx_ref[...])
        log_a = -LRU_C * r * _softplus(-lam_ref[...])
        a = jnp.exp(log_a)
        u = jnp.sqrt(-_expm1(2.0 * log_a)) * (i * xc)
        d = 1
        while d < CHUNK:
            a_sh = jnp.where(row >= d, pltpu.roll(a, d, 0), 1.0)
            u_sh = jnp.where(row >= d, pltpu.roll(u, d, 0), 0.0)
            u = a * u_sh + u
            a = a * a_sh
            d *= 2
        hs = u + a * h_ref[...]
        h_ref[...] = hs[CHUNK - 1:CHUNK, :]
        gate = z_ref[rows, w:2 * w]
        o_ref[rows, :] = hs * jax.nn.gelu(gate, approximate=True)
        yield


def _mla_parts(z_ref, posc_ref, invf_ref, e_ref, base_ref, qn_ref, wq_ref, kvn_ref, wk_ref, wvt_ref,
               o_ref, q_s, k_s, vt_s):
    s = z_ref.shape[0]
    nt = s // ATT_TILE
    exp2_coef = (MLA_NOPE + MLA_ROPE) ** -0.5 * LOG2_E
    lane = lax.broadcasted_iota(jnp.int32, (ROPE_TILE, LANES), 1)
    first = (lane >= MLA_NOPE) & (lane < MLA_NOPE + MLA_ROPE // 2)

    def rope(x, cos, sin_signed):
        return x * cos + _lane_block_swap(x, MLA_ROPE // 2, first) * sin_signed

    def project(t):
        rows = pl.ds(pl.multiple_of(t * ROPE_TILE, ROPE_TILE), ROPE_TILE)
        ang = posc_ref[t] * invf_ref[...]
        cos = _expand_rope_table(jnp.cos(ang), e_ref) + base_ref[...]
        sin = _expand_rope_table(jnp.sin(ang), e_ref)
        yield
        sin_signed = jnp.where(first, -sin, sin)
        cq = (_rms(z_ref[rows, 0:ZC_KV], MLA_Q_LORA) * qn_ref[...]).astype(BF16)
        ckv_f = _rms(z_ref[rows, ZC_KV:ZC_KR], MLA_KV_LORA) * kvn_ref[...]
        ckv = ckv_f.astype(BF16)
        k_rope = rope(z_ref[rows, ZC_KR:ZC_W], cos, sin_signed)
        vt = jnp.dot(wvt_ref[...], ckv_f.T.astype(BF16), preferred_element_type=F32)
        qs = [jnp.dot(cq, wq_ref[h], preferred_element_type=F32) for h in range(HEADS)]
        yield
        ks = [jnp.dot(ckv, wk_ref[h], preferred_element_type=F32) for h in range(HEADS)]
        vt_s[:, rows] = vt.astype(BF16)
        yield
        for h in range(HEADS):
            q_s[h, rows, :] = rope(qs[h], cos, sin_signed).astype(BF16)
            k_s[h, rows, :] = (ks[h] + k_rope).astype(BF16)

    key_id = lax.broadcasted_iota(jnp.int32, (ATT_TILE, ATT_TILE), 0) // CHUNK
    qry_id = lax.broadcasted_iota(jnp.int32, (ATT_TILE, ATT_TILE), 1) // CHUNK
    diag_visible = key_id <= qry_id

    def q_tile(i, carry):
        rows = pl.ds(pl.multiple_of(i * ATT_TILE, ATT_TILE), ATT_TILE)

        def kv_tile(j, state, on_diagonal):
            cols = pl.ds(pl.multiple_of(j * ATT_TILE, ATT_TILE), ATT_TILE)
            scores = [lax.dot_general(k_s[h, cols, :], q_s[h, rows, :], (((1,), (1,)), ((), ())),
                                      preferred_element_type=F32) for h in range(HEADS)]
            new_state = []
            for h in range(HEADS):
                m, l, acc = state[h]
                sc = scores[h]
                if on_diagonal:
                    sc = jnp.where(diag_visible, sc, -jnp.inf)
                m_new = jnp.maximum(m, jnp.max(sc, axis=0, keepdims=True))
                p = jnp.exp2((sc - m_new) * exp2_coef)
                alpha = jnp.exp2((m - m_new) * exp2_coef)
                l = alpha * l + jnp.sum(p, axis=0, keepdims=True)
                vt = vt_s[h * HEAD_DIM:(h + 1) * HEAD_DIM, cols]
                acc = alpha * acc + jnp.dot(vt, p.astype(BF16), preferred_element_type=F32)
                new_state.append((m_new, l, acc))
            return tuple(new_state)

        init = tuple((jnp.full((1, ATT_TILE), -jnp.inf, F32), jnp.zeros((1, ATT_TILE), F32),
                      jnp.zeros((HEAD_DIM, ATT_TILE), F32)) for _ in range(HEADS))
        state = lax.fori_loop(0, i, lambda j, st: kv_tile(j, st, False), init)
        state = kv_tile(i, state, True)
        o_t = jnp.concatenate([acc / l for _, l, acc in state], axis=0)
        o_ref[rows, :] = o_t.T
        return carry

    def attention():
        lax.fori_loop(0, nt, q_tile, 0)

    return project, attention


def _mla_consts(pos_f, q_norm, w_uq, kv_norm, w_ukv):
    h = HEADS
    half = MLA_ROPE // 2
    inv_freq = jnp.power(ROPE_THETA, -jnp.arange(half, dtype=F32) / half)
    invf = jnp.tile(inv_freq, LANES // half)[None, :]
    posc = _compact_positions(pos_f, half)
    expand = _rope_expanders(half, LANES, MLA_NOPE, MLA_NOPE + MLA_ROPE)
    lane = jnp.arange(LANES)
    base = ((lane < MLA_NOPE) | (lane >= MLA_NOPE + MLA_ROPE)).astype(F32)[None, :]
    qn = jnp.zeros((1, ZC_KV), F32).at[0, :MLA_Q_LORA].set(q_norm)
    kvn = kv_norm.reshape(1, MLA_KV_LORA)
    wq = w_uq.reshape(MLA_Q_LORA, h, MLA_NOPE + MLA_ROPE).transpose(1, 0, 2)
    wq = jnp.pad(wq, ((0, 0), (0, ZC_KV - MLA_Q_LORA), (0, LANES - MLA_NOPE - MLA_ROPE))).astype(BF16)
    wkv = w_ukv.reshape(MLA_KV_LORA, h, MLA_NOPE + HEAD_DIM)
    wk = jnp.pad(wkv[:, :, :MLA_NOPE].transpose(1, 0, 2),
                 ((0, 0), (0, 0), (0, LANES - MLA_NOPE))).astype(BF16)
    wvt = wkv[:, :, MLA_NOPE:].reshape(MLA_KV_LORA, h * HEAD_DIM).T.astype(BF16)
    return posc, (invf, expand, base, qn, wq, kvn, wk, wvt)


RET_UNROLL = 2
N_RET_CONSTS = 7
N_MLA_CONSTS = 8


def _ret_mla_kernel(*refs):
    za_ref, pa_ref = refs[0:2]
    ret_consts = refs[2:2 + N_RET_CONSTS]
    zc_ref, pc_ref = refs[2 + N_RET_CONSTS:4 + N_RET_CONSTS]
    mla_consts = refs[4 + N_RET_CONSTS:4 + N_RET_CONSTS + N_MLA_CONSTS]
    oa_ref, oc_ref, state_ref, q_s, k_s, vt_s = refs[4 + N_RET_CONSTS + N_MLA_CONSTS:]
    retention = _retention_stages(za_ref, pa_ref, *ret_consts, oa_ref, state_ref)
    project, attention = _mla_parts(zc_ref, pc_ref, *mla_consts, oc_ref, q_s, k_s, vt_s)

    def groups(t, carry):
        gs = [RET_UNROLL * t + u for u in range(RET_UNROLL)]
        _interleave(*[retention(g) for g in gs], *[project(g) for g in gs])
        return carry

    n_groups = za_ref.shape[0] // ROPE_TILE
    assert n_groups % RET_UNROLL == 0
    lax.fori_loop(0, n_groups // RET_UNROLL, groups, 0)
    attention()


def _ret_mla(za, zc, pos_f, ret_gn, q_norm, w_uq, kv_norm, w_ukv):
    b, s, _ = za.shape
    pos_a, ret_consts = _retention_consts(pos_f, ret_gn)
    pos_c, mla_consts = _mla_consts(pos_f, q_norm, w_uq, kv_norm, w_ukv)
    assert len(ret_consts) == N_RET_CONSTS and len(mla_consts) == N_MLA_CONSTS
    full = lambda a: pl.BlockSpec(a.shape, lambda i: (0,) * a.ndim)
    seq = lambda w: pl.BlockSpec((None, s, w), lambda i: (i, 0, 0))
    tiles = lambda p: pl.BlockSpec((None,) + p.shape[1:], lambda i: (i, 0, 0, 0))
    return pl.pallas_call(
        _ret_mla_kernel,
        grid=(b,),
        in_specs=[seq(ZA_W), tiles(pos_a)] + [full(a) for a in ret_consts]
                 + [seq(ZC_W), tiles(pos_c)] + [full(a) for a in mla_consts],
        out_specs=[seq(GROUP_WIDTH), seq(GROUP_WIDTH)],
        out_shape=[jax.ShapeDtypeStruct((b, s, GROUP_WIDTH), F32)] * 2,
        scratch_shapes=[pltpu.VMEM((GROUP_WIDTH, GROUP_WIDTH), F32),
                        pltpu.VMEM((HEADS, s, LANES), BF16), pltpu.VMEM((HEADS, s, LANES), BF16),
                        pltpu.VMEM((GROUP_WIDTH, s), BF16)],
        compiler_params=pltpu.CompilerParams(dimension_semantics=("parallel",),
                                             vmem_limit_bytes=VMEM_LIMIT),
        name="ret_mla",
    )(za, pos_a, *ret_consts, zc, pos_c, *mla_consts)


def _bf16_parts(x, parts):
    out = []
    for _ in range(parts):
        p = x.astype(BF16)
        out.append(p)
        x = x - p.astype(F32)
    return out


def _dot_exact(a, b, parts=None):
    f32_is_lhs = isinstance(a, list) or a.dtype == F32
    split = a if f32_is_lhs else b
    if not isinstance(split, list):
        split = _bf16_parts(split, parts)
    terms = [jnp.dot(p, b, preferred_element_type=F32) if f32_is_lhs
             else jnp.dot(a, p, preferred_element_type=F32) for p in split]
    acc = terms[-1]
    for t in terms[-2::-1]:
        acc = acc + t
    return acc


def _gdn_lru_kernel(z_ref, cw_ref, alog_ref, dtb_ref, ng_ref,
                    zb_ref, lcw_ref, lcb_ref, lwa_ref, lba_ref, lwx_ref, lbx_ref, llam_ref,
                    o_ref, ob_ref, u_s, wq_s, kd_s, attn_s, gl_s, state_ref, h_ref):
    s = z_ref.shape[0]
    gw = GROUP_WIDTH
    g_rows = GDN_ROWS
    per = g_rows // CHUNK
    ri = lax.broadcasted_iota(jnp.int32, (g_rows, g_rows), 0)
    ci = lax.broadcasted_iota(jnp.int32, (g_rows, g_rows), 1)
    block_mask = lambda width: (ri // width) == (ci // width)
    eye = ri == ci
    same_block = block_mask(CHUNK)
    causal = same_block & (ri >= ci)
    strict = same_block & (ri > ci)
    tri = causal.astype(BF16)
    ones_bd = same_block.astype(BF16)
    er = lax.broadcasted_iota(jnp.int32, (LANES, gw), 0)
    ec = lax.broadcasted_iota(jnp.int32, (LANES, gw), 1) // HEAD_DIM
    expand_beta = (er == ec).astype(BF16)
    expand_alpha = (er == ec + HEADS).astype(BF16)

    def precompute(gi):
        rows = pl.ds(pl.multiple_of(gi * g_rows, g_rows), g_rows)
        qkv = _silu(_causal_conv_rows(z_ref, gi, g_rows, 0, 3 * gw, cw_ref[...]))
        q = qkv[:, 0:gw]
        k = qkv[:, gw:2 * gw]
        v = qkv[:, 2 * gw:3 * gw]
        small = z_ref[rows, 4 * gw:4 * gw + LANES]
        beta_t = jax.nn.sigmoid(small)
        g_t = -jnp.exp(alog_ref[...]) * _softplus(small + dtb_ref[...])
        qn = q * lax.rsqrt(_dot_exact(q * q, ones_bd, 2) + EPS) * (HEAD_DIM ** -0.5)
        kn = k * lax.rsqrt(_dot_exact(k * k, ones_bd, 2) + EPS)
        beta_w = _dot_exact(beta_t, expand_beta, 3)
        g_t_parts = _bf16_parts(g_t, 3)
        g_w = _dot_exact(g_t_parts, expand_alpha)
        gc_t = _dot_exact(tri, g_t_parts)
        yield
        g_w_parts = _bf16_parts(g_w, 3)
        gc_w = _dot_exact(tri, g_w_parts)
        gl_w = _dot_exact(ones_bd, g_w_parts)
        gc_tt = gc_t.T
        kb_w = kn * beta_w
        vb_w = v * beta_w
        heads = [slice(h * HEAD_DIM, (h + 1) * HEAD_DIM) for h in range(HEADS)]
        kh = [kn[:, sl].astype(BF16) for sl in heads]
        grams = [_dot_nt(kb_w[:, sl], kh[h]) for h, sl in enumerate(heads)]
        yield
        qks = [_dot_nt(qn[:, sl], kh[h]) for h, sl in enumerate(heads)]
        egc_w = jnp.exp(gc_w)
        kbe_w = kb_w * egc_w
        yield
        lowers, rhs = [], []
        for h, sl in enumerate(heads):
            gcol = jnp.broadcast_to(gc_t[:, HEADS + h:HEADS + h + 1], (g_rows, g_rows))
            grow = jnp.broadcast_to(gc_tt[HEADS + h:HEADS + h + 1, :], (g_rows, g_rows))
            decay = jnp.exp(jnp.where(causal, gcol - grow, -jnp.inf))
            lowers.append(jnp.where(strict, grams[h] * decay, 0.0))
            attn_s[h, rows, :] = (qks[h] * decay).astype(BF16)
            rhs.append(jnp.concatenate([vb_w[:, sl], kbe_w[:, sl]], axis=1))
        bf = lambda xs: [x.astype(BF16) for x in xs]
        plus_eye = lambda x: jnp.where(eye, x + 1.0, x)
        inner = block_mask(SOLVE_BASE)
        d = [jnp.where(inner, low, 0.0) for low in lowers]
        factors = [[jnp.where(eye, 1.0, -x) for x in d]]
        power = bf(d)
        width = 2
        while width < SOLVE_BASE:
            power_f = [_dot(p, p) for p in power]
            yield
            factors.append([plus_eye(x) for x in power_f])
            power = bf(power_f)
            width *= 2
        while len(factors) > 1:
            factors = [[_dot(x, y) for x, y in zip(factors[i], factors[i + 1])] if i + 1 < len(factors)
                       else factors[i] for i in range(0, len(factors), 2)]
            yield
        t = factors[0]
        width = SOLVE_BASE
        while width < CHUNK:
            outer = block_mask(2 * width)
            e = [jnp.where(outer & ~inner, low, 0.0).astype(BF16) for low in lowers]
            t_b = bf(t)
            et = [_dot(x, y) for x, y in zip(e, t_b)]
            yield
            t = [x - _dot(y, z) for x, y, z in zip(t, t_b, et)]
            yield
            inner = outer
            width *= 2
        sols = [_dot(x, y) for x, y in zip(t, rhs)]
        yield
        u_s[rows, :] = jnp.concatenate([sol[:, :HEAD_DIM] for sol in sols], axis=1)
        w_all = jnp.concatenate([sol[:, HEAD_DIM:] for sol in sols], axis=1).astype(BF16)
        qd_w = (qn * egc_w).astype(BF16)
        kd_s[rows, :] = (kn * jnp.exp(gl_w - gc_w)).astype(BF16)
        for c in range(per):
            blk = slice(c * CHUNK, (c + 1) * CHUNK)
            wq_s[gi * per + c, 0:CHUNK, :] = w_all[blk]
            wq_s[gi * per + c, CHUNK:2 * CHUNK, :] = qd_w[blk]
            gl_s[gi * per + c] = gl_w[c * CHUNK:c * CHUNK + 1, :]

    def recur(gi):
        st = state_ref[...]
        outs = []
        for c in range(per):
            n = gi * per + c
            rows = _chunk_rows(n)
            r1 = _dot(wq_s[n], st)
            yield
            v_new = (u_s[rows, :] - r1[0:CHUNK]).astype(BF16)
            ds = lax.dot_general(kd_s[rows, :], v_new, (((0,), (0,)), ((), ())),
                                 preferred_element_type=F32)
            parts = []
            for h in range(HEADS):
                sl = slice(h * HEAD_DIM, (h + 1) * HEAD_DIM)
                a = attn_s[h, rows, c * CHUNK:(c + 1) * CHUNK]
                parts.append(jnp.dot(a, v_new[:, sl], preferred_element_type=F32))
            yield
            outs.append(r1[CHUNK:2 * CHUNK] + jnp.concatenate(parts, axis=1))
            st = st * jnp.exp(gl_s[n]) + jnp.where(same_block, ds, 0.0)
        state_ref[...] = st
        rows = pl.ds(pl.multiple_of(gi * g_rows, g_rows), g_rows)
        o = jnp.concatenate(outs, axis=0)
        ms = _dot_exact(o * o, ones_bd, 2) * (1.0 / HEAD_DIM)
        gate = z_ref[rows, 3 * gw:4 * gw]
        o_ref[rows, :] = o * lax.rsqrt(ms + EPS) * ng_ref[...] * _silu(gate)

    def lru(gi):
        return _lru_stages(gi, zb_ref, lcw_ref, lcb_ref, lwa_ref, lba_ref, lwx_ref, lbx_ref, llam_ref,
                           ob_ref, h_ref)

    state_ref[...] = jnp.zeros_like(state_ref)
    h_ref[...] = jnp.zeros_like(h_ref)
    n_groups = s // g_rows
    _interleave(precompute(0), lru(0))

    def steady(gi, carry):
        _interleave(recur(gi - 1), precompute(gi), lru(gi))
        return carry

    lax.fori_loop(1, n_groups, steady, 0)
    _interleave(recur(n_groups - 1))


def _gdn_lru(zd, zb, conv_w, a_log, dt_bias, norm_g, lru_conv_w, lru_conv_b, wa, ba, wx, bx, lam):
    b, s, _ = zd.shape
    lane_vec = lambda p: jnp.zeros((1, LANES), F32).at[0, HEADS:2 * HEADS].set(p)
    ng = jnp.tile(norm_g, HEADS)[None, :]
    bd = lambda m: jax.scipy.linalg.block_diag(*[m[i] for i in range(m.shape[0])]).astype(BF16)
    vec = lambda v: v.reshape(1, GROUP_WIDTH)
    gdn_args = (conv_w, lane_vec(a_log), lane_vec(dt_bias), ng)
    lru_args = (lru_conv_w, vec(lru_conv_b), bd(wa), vec(ba), bd(wx), vec(bx), vec(lam))
    full = lambda a: pl.BlockSpec(a.shape, lambda i: (0,) * a.ndim)
    seq = lambda w: pl.BlockSpec((None, s, w), lambda i: (i, 0, 0))
    return pl.pallas_call(
        _gdn_lru_kernel,
        grid=(b,),
        in_specs=[seq(ZD_W)] + [full(a) for a in gdn_args] + [seq(ZB_W)] + [full(a) for a in lru_args],
        out_specs=[seq(GROUP_WIDTH), seq(GROUP_WIDTH)],
        out_shape=[jax.ShapeDtypeStruct((b, s, GROUP_WIDTH), F32)] * 2,
        scratch_shapes=[pltpu.VMEM((s, GROUP_WIDTH), F32),
                        pltpu.VMEM((s // CHUNK, 2 * CHUNK, GROUP_WIDTH), BF16),
                        pltpu.VMEM((s, GROUP_WIDTH), BF16),
                        pltpu.VMEM((HEADS, s, GDN_ROWS), BF16),
                        pltpu.VMEM((s // CHUNK, 1, GROUP_WIDTH), F32),
                        pltpu.VMEM((GROUP_WIDTH, GROUP_WIDTH), F32),
                        pltpu.VMEM((1, GROUP_WIDTH), F32)],
        compiler_params=pltpu.CompilerParams(dimension_semantics=("parallel",),
                                             vmem_limit_bytes=VMEM_LIMIT),
        name="gdn_lru",
    )(zd, *gdn_args, zb, *lru_args)


def _mem_kv_kernel(mem_ref, g_ref, wkv_ref, k_ref, v_ref):
    m = (_rms(mem_ref[...], D_MODEL) * g_ref[...]).astype(BF16)
    k_ref[...] = jnp.dot(m, wkv_ref[:, :D_MODEL], preferred_element_type=F32).astype(BF16)
    v_ref[...] = jnp.dot(m, wkv_ref[:, D_MODEL:], preferred_element_type=F32).astype(BF16)


def _mem_kv(mem, gains, wkv, layer):
    b, m, _ = mem.shape
    blk = pl.BlockSpec((None, m, D_MODEL), lambda i: (i, 0, 0))
    return pl.pallas_call(
        _mem_kv_kernel,
        grid=(b,),
        in_specs=[blk, _layer_spec(gains, layer, 1), _layer_spec(wkv, layer, 1)],
        out_specs=[blk, blk],
        out_shape=[jax.ShapeDtypeStruct((b, m, D_MODEL), BF16)] * 2,
        compiler_params=pltpu.CompilerParams(dimension_semantics=("parallel",),
                                             vmem_limit_bytes=VMEM_LIMIT),
        name="mem_kv",
    )(mem, gains, wkv)


def _token_kernel(x_ref, ya_ref, yb_ref, yc_ref, yd_ref, k_ref, v_ref, wmix_ref, gx_ref, wq_ref, wo_ref,
                  gm_ref, w1_ref, w2_ref, fg_ref, o_ref, *, final_norm):
    def part(rs):
        x = x_ref[rs, :]
        for gi, y_ref in enumerate((ya_ref, yb_ref, yc_ref, yd_ref)):
            x = x + jnp.dot(y_ref[rs, :].astype(BF16), wmix_ref[gi * GROUP_WIDTH:(gi + 1) * GROUP_WIDTH, :],
                            preferred_element_type=F32)
        yield
        hq = (_rms(x, D_MODEL) * gx_ref[...]).astype(BF16)
        q = jnp.dot(hq, wq_ref[...], preferred_element_type=F32).astype(BF16)
        yield
        heads = [slice(h * XA_HEAD_DIM, (h + 1) * XA_HEAD_DIM) for h in range(XA_HEADS)]
        scores = [lax.dot_general(q[:, sl], k_ref[:, sl], (((1,), (1,)), ((), ())),
                                  preferred_element_type=F32) for sl in heads]
        yield
        outs = []
        for h, sl in enumerate(heads):
            sc = scores[h] * (XA_HEAD_DIM ** -0.5)
            e = jnp.exp(sc - jnp.max(sc, axis=-1, keepdims=True))
            p = e / jnp.sum(e, axis=-1, keepdims=True)
            outs.append(jnp.dot(p.astype(BF16), v_ref[:, sl], preferred_element_type=F32).astype(BF16))
        yield
        x = x + jnp.dot(jnp.concatenate(outs, axis=1), wo_ref[...], preferred_element_type=F32)
        yield
        hm = (_rms(x, D_MODEL) * gm_ref[...]).astype(BF16)
        slabs = [slice(c * FF_SLAB, (c + 1) * FF_SLAB) for c in range(D_FF // FF_SLAB)]
        a = jnp.dot(hm, w1_ref[:, slabs[0]], preferred_element_type=F32)
        yield
        for c in range(1, len(slabs)):
            a_next = jnp.dot(hm, w1_ref[:, slabs[c]], preferred_element_type=F32)
            a = jnp.maximum(a, 0.0)
            x = x + jnp.dot((a * a).astype(BF16), w2_ref[slabs[c - 1], :], preferred_element_type=F32)
            a = a_next
            yield
        a = jnp.maximum(a, 0.0)
        x = x + jnp.dot((a * a).astype(BF16), w2_ref[slabs[-1], :], preferred_element_type=F32)
        if final_norm:
            x = _rms(x, D_MODEL) * fg_ref[...]
        o_ref[rs, :] = x

    _interleave(*[part(slice(r, r + TOKEN_SUB)) for r in range(0, TOKEN_BLOCK, TOKEN_SUB)])


def _token_mix(x, ys, k, v, weights, final_gain, layer, final_norm):
    b, s, _ = x.shape
    m = k.shape[1]
    row = lambda w: pl.BlockSpec((None, TOKEN_BLOCK, w), lambda i, j: (i, j, 0))
    mem = pl.BlockSpec((None, m, D_MODEL), lambda i, j: (i, 0, 0))
    const = lambda a: _layer_spec(a, layer, 2, pipeline_mode=pl.Buffered(1))
    final = pl.BlockSpec(final_gain.shape, lambda i, j: (0, 0), pipeline_mode=pl.Buffered(1))
    return pl.pallas_call(
        functools.partial(_token_kernel, final_norm=final_norm),
        grid=(b, s // TOKEN_BLOCK),
        in_specs=[row(D_MODEL)] + [row(GROUP_WIDTH)] * 4 + [mem, mem] + [const(a) for a in weights] + [final],
        out_specs=row(D_MODEL),
        out_shape=jax.ShapeDtypeStruct((b, s, D_MODEL), F32),
        compiler_params=pltpu.CompilerParams(dimension_semantics=("parallel", "parallel"),
                                             vmem_limit_bytes=VMEM_LIMIT),
        name="token_mix",
    )(x, *ys, k, v, *weights, final_gain)


def _pack_w_in(w):
    w = w.astype(BF16)
    zeros = lambda n: jnp.zeros(w.shape[:-1] + (n,), w.dtype)
    c0 = ZA_W + ZB_W
    cq = w[..., c0:c0 + MLA_Q_LORA]
    ckv = w[..., c0 + MLA_Q_LORA:c0 + MLA_Q_LORA + MLA_KV_LORA]
    kr = w[..., c0 + MLA_Q_LORA + MLA_KV_LORA:c0 + MLA_Q_LORA + MLA_KV_LORA + MLA_ROPE]
    d0 = c0 + MLA_Q_LORA + MLA_KV_LORA + MLA_ROPE
    rest = w[..., d0:]
    packed = jnp.concatenate([
        w[..., :c0],
        cq, zeros(ZC_KV - MLA_Q_LORA), ckv, zeros(MLA_NOPE), kr, zeros(LANES - MLA_NOPE - MLA_ROPE),
        rest, zeros(ZD_W - rest.shape[-1]),
    ], axis=-1)
    assert packed.shape[-1] == Z_W
    return packed.astype(BF16)


def kernel(x, mem, positions, norm_mix, w_in, ret_gn, lru_conv_w, lru_conv_b, lru_wa, lru_ba, lru_wx, lru_bx, lru_lambda, mla_q_norm, mla_w_uq, mla_kv_norm, mla_w_ukv, gdn_conv_w, gdn_a_log, gdn_dt_bias, gdn_norm, w_out, norm_xattn, norm_mem, xa_wq, xa_wkv, xa_wo, norm_mlp, mlp_w1, mlp_w2, final_norm):
    b, s, d = x.shape
    depth = w_in.shape[0]
    pos_f = positions.astype(F32)
    rows = lambda g: g[:, None, :]
    w_in_packed = _pack_w_in(w_in)
    wkv = xa_wkv.astype(BF16)
    token_weights = (w_out.astype(BF16), rows(norm_xattn), xa_wq.astype(BF16), xa_wo.astype(BF16),
                     rows(norm_mlp), mlp_w1.astype(BF16), mlp_w2.astype(BF16))
    for l in range(depth):
        za, zb, zc, zd = _in_proj(x.reshape(b * s, d), rows(norm_mix), w_in_packed, l)
        seq = lambda z: z.reshape(b, s, z.shape[-1])
        y_a, y_c = _ret_mla(seq(za), seq(zc), pos_f, ret_gn[l], mla_q_norm[l], mla_w_uq[l], mla_kv_norm[l],
                            mla_w_ukv[l])
        y_d, y_b = _gdn_lru(seq(zd), seq(zb), gdn_conv_w[l], gdn_a_log[l], gdn_dt_bias[l], gdn_norm[l],
                            lru_conv_w[l], lru_conv_b[l], lru_wa[l], lru_ba[l], lru_wx[l], lru_bx[l],
                            lru_lambda[l])
        k, v = _mem_kv(mem, rows(norm_mem), wkv, l)
        x = _token_mix(x, (y_a, y_b, y_c, y_d), k, v, token_weights, final_norm[None, :], l, l == depth - 1)
    return x
```

```python
import functools

import jax
import jax.numpy as jnp
from jax import lax
from jax.experimental import pallas as pl
from jax.experimental.pallas import tpu as pltpu

F32 = jnp.float32
BF16 = jnp.bfloat16

D_MODEL = 1024
CHUNK = 64
HEADS = 4
HEAD_DIM = 64
GROUP_WIDTH = HEADS * HEAD_DIM
EPS = 1e-6
LOG2_E = 1.4426950408889634
ROPE_THETA = 10000.0
LRU_C = 8.0
MLA_Q_LORA = 192
MLA_KV_LORA = 128
MLA_ROPE = 32
MLA_NOPE = 64
XA_HEADS = 4
XA_HEAD_DIM = D_MODEL // XA_HEADS
D_FF = 4 * D_MODEL
FF_SLAB = 1024
ATT_TILE = 512
ROPE_TILE = 256
GDN_ROWS = 256
SOLVE_BASE = 16
assert GDN_ROWS == GROUP_WIDTH and HEAD_DIM == CHUNK
ROW_TILE = 512
TOKEN_BLOCK = 512
TOKEN_SUB = 256
SUBLANES = 8
LANES = 128
VMEM_LIMIT = 56 * 1024 * 1024

ZA_W = 4 * GROUP_WIDTH
ZB_W = 2 * GROUP_WIDTH
ZC_W = 512
ZD_W = 1152
ZC_KV = 256
ZC_KR = 384
Z_W = ZA_W + ZB_W + ZC_W + ZD_W


def _dot(a, b):
    return jnp.dot(a.astype(BF16), b.astype(BF16), preferred_element_type=F32)


def _dot_nt(a, b):
    return lax.dot_general(a.astype(BF16), b.astype(BF16), (((1,), (1,)), ((), ())),
                           preferred_element_type=F32)


def _dot_tn(a, b):
    return lax.dot_general(a.astype(BF16), b.astype(BF16), (((0,), (0,)), ((), ())),
                           preferred_element_type=F32)


def _silu(x):
    return x * jax.nn.sigmoid(x)


def _softplus(x):
    return jnp.maximum(x, 0.0) + jnp.log1p(jnp.exp(-jnp.abs(x)))


def _expm1(y):
    e = jnp.exp(y)
    near = (y > -0.5) & (e != 1.0)
    safe_log = jnp.where(near, jnp.log(e), 1.0)
    return jnp.where(near, (e - 1.0) * y / safe_log, jnp.where(e == 1.0, y, e - 1.0))


def _rms(x, n):
    return x * lax.rsqrt(jnp.sum(x * x, axis=-1, keepdims=True) * (1.0 / n) + EPS)


def _layer_spec(stacked, layer, grid_rank, **kwargs):
    zeros = (0,) * (stacked.ndim - 1)
    index_map = (lambda i: (layer,) + zeros) if grid_rank == 1 else (lambda i, j: (layer,) + zeros)
    return pl.BlockSpec((None,) + stacked.shape[1:], index_map, **kwargs)


def _chunk_rows(n):
    return pl.ds(pl.multiple_of(n * CHUNK, CHUNK), CHUNK)


def _causal_conv_rows(z_ref, n, nrows, lo, hi, w):
    cur = z_ref[pl.ds(pl.multiple_of(n * nrows, nrows), nrows), lo:hi]
    prev_start = pl.multiple_of(jnp.maximum(n * nrows - SUBLANES, 0), SUBLANES)
    prev = z_ref[pl.ds(prev_start, SUBLANES), lo:hi]
    prev = jnp.where(n > 0, prev, 0.0)
    tile = jnp.concatenate([prev, cur], axis=0)
    acc = cur * w[3:4, :]
    for k in (1, 2, 3):
        acc = acc + pltpu.roll(tile, k, 0)[SUBLANES:, :] * w[3 - k:4 - k, :]
    return acc


def _lane_block_swap(x, half, first):
    return jnp.where(first, pltpu.roll(x, LANES - half, 1), pltpu.roll(x, half, 1))


def _in_proj_kernel(x_ref, g_ref, w_ref, za_ref, zb_ref, zc_ref, zd_ref):
    x = x_ref[...]
    h = (_rms(x, D_MODEL) * g_ref[...]).astype(BF16)
    lo = 0
    for ref in (za_ref, zb_ref, zc_ref, zd_ref):
        hi = lo + ref.shape[-1]
        ref[...] = jnp.dot(h, w_ref[:, lo:hi], preferred_element_type=F32)
        lo = hi


def _in_proj(x2d, gains, w_packed, layer):
    t = x2d.shape[0]
    row = lambda w: pl.BlockSpec((ROW_TILE, w), lambda i: (i, 0))
    return pl.pallas_call(
        _in_proj_kernel,
        grid=(t // ROW_TILE,),
        in_specs=[row(D_MODEL), _layer_spec(gains, layer, 1), _layer_spec(w_packed, layer, 1)],
        out_specs=[row(ZA_W), row(ZB_W), row(ZC_W), row(ZD_W)],
        out_shape=[jax.ShapeDtypeStruct((t, w), F32) for w in (ZA_W, ZB_W, ZC_W, ZD_W)],
        compiler_params=pltpu.CompilerParams(dimension_semantics=("parallel",),
                                             vmem_limit_bytes=VMEM_LIMIT),
        name="in_proj",
    )(x2d, gains, w_packed)


def _compact_positions(pos_f, n_freq):
    b, s = pos_f.shape
    nb = LANES // n_freq
    r = ROPE_TILE // nb
    p = pos_f.reshape(b, s // ROPE_TILE, nb, r).transpose(0, 1, 3, 2)
    return jnp.repeat(p, n_freq, axis=-1)


def _rope_expanders(n_freq, width, rope_lo, rope_hi):
    nb = LANES // n_freq
    src = jnp.arange(LANES)
    lane = jnp.arange(width)
    on_rope = (lane >= rope_lo) & (lane < rope_hi)
    hit = (src[:, None] % n_freq == lane[None, :] % n_freq) & on_rope[None, :]
    return jnp.stack([(hit & (src[:, None] // n_freq == r)) for r in range(nb)]).astype(BF16)


def _expand_rope_table(t_c, e_ref):
    return jnp.concatenate([_dot_exact(t_c, e_ref[r], 3) for r in range(e_ref.shape[0])], axis=0)


def _retention_stages(z_ref, posc_ref, invf_ref, e_ref, dmat_ref, xi_ref, zeta_ref, cd_ref, gn_ref,
                      o_ref, state_ref):
    gw = GROUP_WIDTH
    n_rows = ROPE_TILE
    lane = lax.broadcasted_iota(jnp.int32, (n_rows, LANES), 1)
    first = (lane % HEAD_DIM) < (HEAD_DIM // 2)
    lane_wide = lax.broadcasted_iota(jnp.int32, (n_rows, gw), 1)
    first_wide = (lane_wide % HEAD_DIM) < (HEAD_DIM // 2)
    ri = lax.broadcasted_iota(jnp.int32, (gw, gw), 0)
    ci = lax.broadcasted_iota(jnp.int32, (gw, gw), 1)
    same_head = (ri // HEAD_DIM) == (ci // HEAD_DIM)
    ones_bd = same_head.astype(BF16)
    state_ref[...] = jnp.zeros_like(state_ref)

    def rope(x, cos, sin_signed):
        parts = []
        for c in range(gw // LANES):
            sl = slice(c * LANES, (c + 1) * LANES)
            xs = _lane_block_swap(x[:, sl], HEAD_DIM // 2, first)
            parts.append(x[:, sl] * cos[:, sl] + xs * sin_signed[:, sl])
        return jnp.concatenate(parts, axis=1)

    def body(g):
        rows = pl.ds(pl.multiple_of(g * n_rows, n_rows), n_rows)
        ang = posc_ref[g] * invf_ref[...]
        cos = _expand_rope_table(jnp.cos(ang), e_ref)
        sin = _expand_rope_table(jnp.sin(ang), e_ref)
        yield
        sin_signed = jnp.where(first_wide, -sin, sin)
        q = (rope(z_ref[rows, 0:gw], cos, sin_signed) * (HEAD_DIM ** -0.5)).astype(BF16)
        k = rope(z_ref[rows, gw:2 * gw], cos, sin_signed)
        v = z_ref[rows, 2 * gw:3 * gw].astype(BF16)
        kz = (k * zeta_ref[...]).astype(BF16)
        k = k.astype(BF16)
        heads = [slice(h * HEAD_DIM, (h + 1) * HEAD_DIM) for h in range(HEADS)]
        blocks = [slice(c * CHUNK, (c + 1) * CHUNK) for c in range(n_rows // CHUNK)]
        scores = [_dot_nt(q[:, sl], k[:, sl]) for sl in heads]
        yield
        kvs = [_dot_tn(kz[blk], v[blk]) for blk in blocks]
        yield
        intra = [_dot(scores[h] * dmat_ref[h], v[:, sl])
                 for h, sl in enumerate(heads)]
        yield
        st = state_ref[...]
        cross = []
        for c, blk in enumerate(blocks):
            cross.append(_dot(q[blk], st))
            st = st * cd_ref[...] + jnp.where(same_head, kvs[c], 0.0)
        state_ref[...] = st
        yield
        o = jnp.concatenate(intra, axis=1) + jnp.concatenate(cross, axis=0) * xi_ref[...]
        mu = _dot_exact(o, ones_bd, 2) * (1.0 / HEAD_DIM)
        yield
        d = o - mu
        var = _dot_exact(d * d, ones_bd, 2) * (1.0 / HEAD_DIM)
        gate = z_ref[rows, 3 * gw:4 * gw]
        o_ref[rows, :] = d * lax.rsqrt(var + EPS) * gn_ref[...] * _silu(gate)

    return body


def _retention_consts(pos_f, ret_gn):
    h = HEADS
    half = HEAD_DIM // 2
    per = ROPE_TILE // CHUNK
    inv_freq = jnp.power(ROPE_THETA, -jnp.arange(half, dtype=F32) / half)
    invf = jnp.tile(inv_freq, LANES // half)[None, :]
    posc = _compact_positions(pos_f, half)
    expand = _rope_expanders(half, GROUP_WIDTH, 0, GROUP_WIDTH)
    log_gamma = jnp.log1p(-jnp.exp2(-5.0 - jnp.arange(h, dtype=F32)))
    idx = jnp.arange(CHUNK, dtype=F32)
    dmat = jnp.exp(jnp.abs(idx[:, None] - idx[None, :])[None] * log_gamma[:, None, None])
    dmat = jnp.stack([jnp.kron(jnp.eye(per, dtype=F32), dmat[i]) for i in range(h)])
    xi = jnp.repeat(jnp.exp((idx[:, None] + 1.0) * log_gamma), HEAD_DIM, axis=1)
    zeta = jnp.repeat(jnp.exp((CHUNK - 1.0 - idx)[:, None] * log_gamma), HEAD_DIM, axis=1)
    xi = jnp.tile(xi, (per, 1))
    zeta = jnp.tile(zeta, (per, 1))
    cd = jnp.repeat(jnp.exp(CHUNK * log_gamma), HEAD_DIM)[None, :]
    gn = ret_gn.reshape(1, GROUP_WIDTH)
    return posc, (invf, expand, dmat, xi, zeta, cd, gn)


def _interleave(*stage_generators):
    live = list(stage_generators)
    while live:
        for gen in list(live):
            try:
                next(gen)
            except StopIteration:
                live.remove(gen)


def _lru_stages(gi, z_ref, cw_ref, cb_ref, wa_ref, ba_ref, wx_ref, bx_ref, lam_ref, o_ref, h_ref):
    w = GROUP_WIDTH
    row = lax.broadcasted_iota(jnp.int32, (CHUNK, w), 0)
    for c in range(GDN_ROWS // CHUNK):
        n = gi * (GDN_ROWS // CHUNK) + c
        rows = _chunk_rows(n)
        xc = _causal_conv_rows(z_ref, n, CHUNK, 0, w, cw_ref[...]) + cb_ref[...]
        r = jax.nn.sigmoid(_dot(xc, wa_ref[...]) + ba_ref[...])
        i = jax.nn.sigmoid(_dot(xc, wx_ref[...]) + bx_ref[...])
        log_a = -LRU_C * r * _softplus(-lam_ref[...])
        a = jnp.exp(log_a)
        u = jnp.sqrt(-_expm1(2.0 * log_a)) * (i * xc)
        d = 1
        while d < CHUNK:
            a_sh = jnp.where(row >= d, pltpu.roll(a, d, 0), 1.0)
            u_sh = jnp.where(row >= d, pltpu.roll(u, d, 0), 0.0)
            u = a * u_sh + u
            a = a * a_sh
            d *= 2
        hs = u + a * h_ref[...]
        h_ref[...] = hs[CHUNK - 1:CHUNK, :]
        gate = z_ref[rows, w:2 * w]
        o_ref[rows, :] = hs * jax.nn.gelu(gate, approximate=True)
        yield


def _mla_parts(z_ref, posc_ref, invf_ref, e_ref, base_ref, qn_ref, wq_ref, kvn_ref, wk_ref, wvt_ref,
               o_ref, q_s, k_s, vt_s):
    s = z_ref.shape[0]
    nt = s // ATT_TILE
    exp2_coef = (MLA_NOPE + MLA_ROPE) ** -0.5 * LOG2_E
    lane = lax.broadcasted_iota(jnp.int32, (ROPE_TILE, LANES), 1)
    first = (lane >= MLA_NOPE) & (lane < MLA_NOPE + MLA_ROPE // 2)

    def rope(x, cos, sin_signed):
        return x * cos + _lane_block_swap(x, MLA_ROPE // 2, first) * sin_signed

    def project(t):
        rows = pl.ds(pl.multiple_of(t * ROPE_TILE, ROPE_TILE), ROPE_TILE)
        ang = posc_ref[t] * invf_ref[...]
        cos = _expand_rope_table(jnp.cos(ang), e_ref) + base_ref[...]
        sin = _expand_rope_table(jnp.sin(ang), e_ref)
        yield
        sin_signed = jnp.where(first, -sin, sin)
        cq = (_rms(z_ref[rows, 0:ZC_KV], MLA_Q_LORA) * qn_ref[...]).astype(BF16)
        ckv_f = _rms(z_ref[rows, ZC_KV:ZC_KR], MLA_KV_LORA) * kvn_ref[...]
        ckv = ckv_f.astype(BF16)
        k_rope = rope(z_ref[rows, ZC_KR:ZC_W], cos, sin_signed)
        vt = jnp.dot(wvt_ref[...], ckv_f.T.astype(BF16), preferred_element_type=F32)
        qs = [jnp.dot(cq, wq_ref[h], preferred_element_type=F32) for h in range(HEADS)]
        yield
        ks = [jnp.dot(ckv, wk_ref[h], preferred_element_type=F32) for h in range(HEADS)]
        vt_s[:, rows] = vt.astype(BF16)
        yield
        for h in range(HEADS):
            q_s[h, rows, :] = rope(qs[h], cos, sin_signed).astype(BF16)
            k_s[h, rows, :] = (ks[h] + k_rope).astype(BF16)

    key_id = lax.broadcasted_iota(jnp.int32, (ATT_TILE, ATT_TILE), 0) // CHUNK
    qry_id = lax.broadcasted_iota(jnp.int32, (ATT_TILE, ATT_TILE), 1) // CHUNK
    diag_visible = key_id <= qry_id

    def q_tile(i, carry):
        rows = pl.ds(pl.multiple_of(i * ATT_TILE, ATT_TILE), ATT_TILE)

        def kv_tile(j, state, on_diagonal):
            cols = pl.ds(pl.multiple_of(j * ATT_TILE, ATT_TILE), ATT_TILE)
            scores = [lax.dot_general(k_s[h, cols, :], q_s[h, rows, :], (((1,), (1,)), ((), ())),
                                      preferred_element_type=F32) for h in range(HEADS)]
            new_state = []
            for h in range(HEADS):
                m, l, acc = state[h]
                sc = scores[h]
                if on_diagonal:
                    sc = jnp.where(diag_visible, sc, -jnp.inf)
                m_new = jnp.maximum(m, jnp.max(sc, axis=0, keepdims=True))
                p = jnp.exp2((sc - m_new) * exp2_coef)
                alpha = jnp.exp2((m - m_new) * exp2_coef)
                l = alpha * l + jnp.sum(p, axis=0, keepdims=True)
                vt = vt_s[h * HEAD_DIM:(h + 1) * HEAD_DIM, cols]
                acc = alpha * acc + jnp.dot(vt, p.astype(BF16), preferred_element_type=F32)
                new_state.append((m_new, l, acc))
            return tuple(new_state)

        init = tuple((jnp.full((1, ATT_TILE), -jnp.inf, F32), jnp.zeros((1, ATT_TILE), F32),
                      jnp.zeros((HEAD_DIM, ATT_TILE), F32)) for _ in range(HEADS))
        state = lax.fori_loop(0, i, lambda j, st: kv_tile(j, st, False), init)
        state = kv_tile(i, state, True)
        o_t = jnp.concatenate([acc / l for _, l, acc in state], axis=0)
        o_ref[rows, :] = o_t.T
        return carry

    def attention():
        lax.fori_loop(0, nt, q_tile, 0)

    return project, attention


def _mla_consts(pos_f, q_norm, w_uq, kv_norm, w_ukv):
    h = HEADS
    half = MLA_ROPE // 2
    inv_freq = jnp.power(ROPE_THETA, -jnp.arange(half, dtype=F32) / half)
    invf = jnp.tile(inv_freq, LANES // half)[None, :]
    posc = _compact_positions(pos_f, half)
    expand = _rope_expanders(half, LANES, MLA_NOPE, MLA_NOPE + MLA_ROPE)
    lane = jnp.arange(LANES)
    base = ((lane < MLA_NOPE) | (lane >= MLA_NOPE + MLA_ROPE)).astype(F32)[None, :]
    qn = jnp.pad(q_norm, (0, ZC_KV - MLA_Q_LORA))[None, :]
    kvn = kv_norm.reshape(1, MLA_KV_LORA)
    wq = w_uq.reshape(MLA_Q_LORA, h, MLA_NOPE + MLA_ROPE).transpose(1, 0, 2)
    wq = jnp.pad(wq, ((0, 0), (0, ZC_KV - MLA_Q_LORA), (0, LANES - MLA_NOPE - MLA_ROPE))).astype(BF16)
    wkv = w_ukv.reshape(MLA_KV_LORA, h, MLA_NOPE + HEAD_DIM)
    wk = jnp.pad(wkv[:, :, :MLA_NOPE].transpose(1, 0, 2),
                 ((0, 0), (0, 0), (0, LANES - MLA_NOPE))).astype(BF16)
    wvt = wkv[:, :, MLA_NOPE:].reshape(MLA_KV_LORA, h * HEAD_DIM).T.astype(BF16)
    return posc, (invf, expand, base, qn, wq, kvn, wk, wvt)


RET_UNROLL = 4
N_RET_CONSTS = 7
N_MLA_CONSTS = 8


def _ret_mla_kernel(*refs):
    za_ref, pa_ref = refs[0:2]
    ret_consts = refs[2:2 + N_RET_CONSTS]
    zc_ref, pc_ref = refs[2 + N_RET_CONSTS:4 + N_RET_CONSTS]
    mla_consts = refs[4 + N_RET_CONSTS:4 + N_RET_CONSTS + N_MLA_CONSTS]
    oa_ref, oc_ref, state_ref, q_s, k_s, vt_s = refs[4 + N_RET_CONSTS + N_MLA_CONSTS:]
    retention = _retention_stages(za_ref, pa_ref, *ret_consts, oa_ref, state_ref)
    project, attention = _mla_parts(zc_ref, pc_ref, *mla_consts, oc_ref, q_s, k_s, vt_s)

    def groups(t, carry):
        gs = [RET_UNROLL * t + u for u in range(RET_UNROLL)]
        _interleave(*[retention(g) for g in gs], *[project(g) for g in gs])
        return carry

    n_groups = za_ref.shape[0] // ROPE_TILE
    assert n_groups % RET_UNROLL == 0
    lax.fori_loop(0, n_groups // RET_UNROLL, groups, 0)
    attention()


def _ret_mla(za, zc, pos_f, ret_gn, q_norm, w_uq, kv_norm, w_ukv):
    b, s, _ = za.shape
    pos_a, ret_consts = _retention_consts(pos_f, ret_gn)
    pos_c, mla_consts = _mla_consts(pos_f, q_norm, w_uq, kv_norm, w_ukv)
    assert len(ret_consts) == N_RET_CONSTS and len(mla_consts) == N_MLA_CONSTS
    full = lambda a: pl.BlockSpec(a.shape, lambda i: (0,) * a.ndim)
    seq = lambda w: pl.BlockSpec((None, s, w), lambda i: (i, 0, 0))
    tiles = lambda p: pl.BlockSpec((None,) + p.shape[1:], lambda i: (i, 0, 0, 0))
    return pl.pallas_call(
        _ret_mla_kernel,
        grid=(b,),
        in_specs=[seq(ZA_W), tiles(pos_a)] + [full(a) for a in ret_consts]
                 + [seq(ZC_W), tiles(pos_c)] + [full(a) for a in mla_consts],
        out_specs=[seq(GROUP_WIDTH), seq(GROUP_WIDTH)],
        out_shape=[jax.ShapeDtypeStruct((b, s, GROUP_WIDTH), F32)] * 2,
        scratch_shapes=[pltpu.VMEM((GROUP_WIDTH, GROUP_WIDTH), F32),
                        pltpu.VMEM((HEADS, s, LANES), BF16), pltpu.VMEM((HEADS, s, LANES), BF16),
                        pltpu.VMEM((GROUP_WIDTH, s), BF16)],
        compiler_params=pltpu.CompilerParams(dimension_semantics=("parallel",),
                                             vmem_limit_bytes=VMEM_LIMIT),
        name="ret_mla",
    )(za, pos_a, *ret_consts, zc, pos_c, *mla_consts)


def _bf16_parts(x, parts):
    out = []
    for _ in range(parts):
        p = x.astype(BF16)
        out.append(p)
        x = x - p.astype(F32)
    return out


def _dot_exact(a, b, parts=None):
    f32_is_lhs = isinstance(a, list) or a.dtype == F32
    split = a if f32_is_lhs else b
    if not isinstance(split, list):
        split = _bf16_parts(split, parts)
    terms = [jnp.dot(p, b, preferred_element_type=F32) if f32_is_lhs
             else jnp.dot(a, p, preferred_element_type=F32) for p in split]
    acc = terms[-1]
    for t in terms[-2::-1]:
        acc = acc + t
    return acc


def _gdn_lru_kernel(z_ref, cw_ref, alog_ref, dtb_ref, ng_ref,
                    zb_ref, lcw_ref, lcb_ref, lwa_ref, lba_ref, lwx_ref, lbx_ref, llam_ref,
                    o_ref, ob_ref, u_s, wq_s, kd_s, attn_s, gl_s, state_ref, h_ref):
    s = z_ref.shape[0]
    gw = GROUP_WIDTH
    g_rows = GDN_ROWS
    per = g_rows // CHUNK
    ri = lax.broadcasted_iota(jnp.int32, (g_rows, g_rows), 0)
    ci = lax.broadcasted_iota(jnp.int32, (g_rows, g_rows), 1)
    block_mask = lambda width: (ri // width) == (ci // width)
    eye = ri == ci
    same_block = block_mask(CHUNK)
    causal = same_block & (ri >= ci)
    strict = same_block & (ri > ci)
    tri = causal.astype(BF16)
    ones_bd = same_block.astype(BF16)
    er = lax.broadcasted_iota(jnp.int32, (LANES, gw), 0)
    ec = lax.broadcasted_iota(jnp.int32, (LANES, gw), 1) // HEAD_DIM
    expand_beta = (er == ec).astype(BF16)
    expand_alpha = (er == ec + HEADS).astype(BF16)

    def precompute(gi):
        rows = pl.ds(pl.multiple_of(gi * g_rows, g_rows), g_rows)
        qkv = _silu(_causal_conv_rows(z_ref, gi, g_rows, 0, 3 * gw, cw_ref[...]))
        q = qkv[:, 0:gw]
        k = qkv[:, gw:2 * gw]
        v = qkv[:, 2 * gw:3 * gw]
        small = z_ref[rows, 4 * gw:4 * gw + LANES]
        beta_t = jax.nn.sigmoid(small)
        g_t = -jnp.exp(alog_ref[...]) * _softplus(small + dtb_ref[...])
        qn = q * lax.rsqrt(_dot_exact(q * q, ones_bd, 2) + EPS) * (HEAD_DIM ** -0.5)
        kn = k * lax.rsqrt(_dot_exact(k * k, ones_bd, 2) + EPS)
        beta_w = _dot_exact(beta_t, expand_beta, 3)
        g_t_parts = _bf16_parts(g_t, 3)
        g_w = _dot_exact(g_t_parts, expand_alpha)
        gc_t = _dot_exact(tri, g_t_parts)
        yield
        g_w_parts = _bf16_parts(g_w, 3)
        gc_w = _dot_exact(tri, g_w_parts)
        gl_w = _dot_exact(ones_bd, g_w_parts)
        gc_tt = gc_t.T
        kb_w = kn * beta_w
        vb_w = v * beta_w
        heads = [slice(h * HEAD_DIM, (h + 1) * HEAD_DIM) for h in range(HEADS)]
        kh = [kn[:, sl].astype(BF16) for sl in heads]
        grams = [_dot_nt(kb_w[:, sl], kh[h]) for h, sl in enumerate(heads)]
        yield
        qks = [_dot_nt(qn[:, sl], kh[h]) for h, sl in enumerate(heads)]
        egc_w = jnp.exp(gc_w)
        kbe_w = kb_w * egc_w
        yield
        lowers, rhs = [], []
        for h, sl in enumerate(heads):
            gcol = jnp.broadcast_to(gc_t[:, HEADS + h:HEADS + h + 1], (g_rows, g_rows))
            grow = jnp.broadcast_to(gc_tt[HEADS + h:HEADS + h + 1, :], (g_rows, g_rows))
            decay = jnp.exp(jnp.where(causal, gcol - grow, -jnp.inf))
            lowers.append(jnp.where(strict, grams[h] * decay, 0.0))
            attn_s[h, rows, :] = (qks[h] * decay).astype(BF16)
            rhs.append(jnp.concatenate([vb_w[:, sl], kbe_w[:, sl]], axis=1))
        bf = lambda xs: [x.astype(BF16) for x in xs]
        plus_eye = lambda x: jnp.where(eye, x + 1.0, x)
        inner = block_mask(SOLVE_BASE)
        d = [jnp.where(inner, low, 0.0) for low in lowers]
        factors = [[jnp.where(eye, 1.0, -x) for x in d]]
        power = bf(d)
        width = 2
        while width < SOLVE_BASE:
            power_f = [_dot(p, p) for p in power]
            yield
            factors.append([plus_eye(x) for x in power_f])
            power = bf(power_f)
            width *= 2
        while len(factors) > 1:
            factors = [[_dot(x, y) for x, y in zip(factors[i], factors[i + 1])] if i + 1 < len(factors)
                       else factors[i] for i in range(0, len(factors), 2)]
            yield
        t = factors[0]
        width = SOLVE_BASE
        while width < CHUNK:
            outer = block_mask(2 * width)
            e = [jnp.where(outer & ~inner, low, 0.0).astype(BF16) for low in lowers]
            t_b = bf(t)
            et = [_dot(x, y) for x, y in zip(e, t_b)]
            yield
            t = [x - _dot(y, z) for x, y, z in zip(t, t_b, et)]
            yield
            inner = outer
            width *= 2
        sols = [_dot(x, y) for x, y in zip(t, rhs)]
        yield
        u_s[rows, :] = jnp.concatenate([sol[:, :HEAD_DIM] for sol in sols], axis=1)
        w_all = jnp.concatenate([sol[:, HEAD_DIM:] for sol in sols], axis=1).astype(BF16)
        qd_w = (qn * egc_w).astype(BF16)
        kd_s[rows, :] = (kn * jnp.exp(gl_w - gc_w)).astype(BF16)
        for c in range(per):
            blk = slice(c * CHUNK, (c + 1) * CHUNK)
            wq_s[gi * per + c, 0:CHUNK, :] = w_all[blk]
            wq_s[gi * per + c, CHUNK:2 * CHUNK, :] = qd_w[blk]
            gl_s[gi * per + c] = gl_w[c * CHUNK:c * CHUNK + 1, :]

    def recur(gi):
        st = state_ref[...]
        outs = []
        for c in range(per):
            n = gi * per + c
            rows = _chunk_rows(n)
            r1 = _dot(wq_s[n], st)
            yield
            v_new = (u_s[rows, :] - r1[0:CHUNK]).astype(BF16)
            ds = lax.dot_general(kd_s[rows, :], v_new, (((0,), (0,)), ((), ())),
                                 preferred_element_type=F32)
            parts = []
            for h in range(HEADS):
                sl = slice(h * HEAD_DIM, (h + 1) * HEAD_DIM)
                a = attn_s[h, rows, c * CHUNK:(c + 1) * CHUNK]
                parts.append(jnp.dot(a, v_new[:, sl], preferred_element_type=F32))
            yield
            outs.append(r1[CHUNK:2 * CHUNK] + jnp.concatenate(parts, axis=1))
            st = st * jnp.exp(gl_s[n]) + jnp.where(same_block, ds, 0.0)
        state_ref[...] = st
        rows = pl.ds(pl.multiple_of(gi * g_rows, g_rows), g_rows)
        o = jnp.concatenate(outs, axis=0)
        ms = _dot_exact(o * o, ones_bd, 2) * (1.0 / HEAD_DIM)
        gate = z_ref[rows, 3 * gw:4 * gw]
        o_ref[rows, :] = o * lax.rsqrt(ms + EPS) * ng_ref[...] * _silu(gate)

    def lru(gi):
        return _lru_stages(gi, zb_ref, lcw_ref, lcb_ref, lwa_ref, lba_ref, lwx_ref, lbx_ref, llam_ref,
                           ob_ref, h_ref)

    state_ref[...] = jnp.zeros_like(state_ref)
    h_ref[...] = jnp.zeros_like(h_ref)
    n_groups = s // g_rows
    _interleave(precompute(0))

    def steady(gi, carry):
        _interleave(recur(gi - 1), precompute(gi), lru(gi - 1))
        return carry

    lax.fori_loop(1, n_groups, steady, 0)
    _interleave(recur(n_groups - 1), lru(n_groups - 1))


def _gdn_lru(zd, zb, conv_w, a_log, dt_bias, norm_g, lru_conv_w, lru_conv_b, wa, ba, wx, bx, lam):
    b, s, _ = zd.shape
    lane_vec = lambda p: jnp.pad(p, (HEADS, LANES - 2 * HEADS))[None, :]
    ng = jnp.tile(norm_g, HEADS)[None, :]
    eye = jnp.eye(wa.shape[0], dtype=wa.dtype)
    bd = lambda m: jnp.einsum("nij,nm->nimj", m, eye).reshape(GROUP_WIDTH, GROUP_WIDTH).astype(BF16)
    vec = lambda v: v.reshape(1, GROUP_WIDTH)
    gdn_args = (conv_w, lane_vec(a_log), lane_vec(dt_bias), ng)
    lru_args = (lru_conv_w, vec(lru_conv_b), bd(wa), vec(ba), bd(wx), vec(bx), vec(lam))
    full = lambda a: pl.BlockSpec(a.shape, lambda i: (0,) * a.ndim)
    seq = lambda w: pl.BlockSpec((None, s, w), lambda i: (i, 0, 0))
    return pl.pallas_call(
        _gdn_lru_kernel,
        grid=(b,),
        in_specs=[seq(ZD_W)] + [full(a) for a in gdn_args] + [seq(ZB_W)] + [full(a) for a in lru_args],
        out_specs=[seq(GROUP_WIDTH), seq(GROUP_WIDTH)],
        out_shape=[jax.ShapeDtypeStruct((b, s, GROUP_WIDTH), F32)] * 2,
        scratch_shapes=[pltpu.VMEM((s, GROUP_WIDTH), F32),
                        pltpu.VMEM((s // CHUNK, 2 * CHUNK, GROUP_WIDTH), BF16),
                        pltpu.VMEM((s, GROUP_WIDTH), BF16),
                        pltpu.VMEM((HEADS, s, GDN_ROWS), BF16),
                        pltpu.VMEM((s // CHUNK, 1, GROUP_WIDTH), F32),
                        pltpu.VMEM((GROUP_WIDTH, GROUP_WIDTH), F32),
                        pltpu.VMEM((1, GROUP_WIDTH), F32)],
        compiler_params=pltpu.CompilerParams(dimension_semantics=("parallel",),
                                             vmem_limit_bytes=VMEM_LIMIT),
        name="gdn_lru",
    )(zd, *gdn_args, zb, *lru_args)


def _mem_kv_kernel(mem_ref, g_ref, wkv_ref, k_ref, v_ref):
    m = (_rms(mem_ref[...], D_MODEL) * g_ref[...]).astype(BF16)
    k_ref[...] = jnp.dot(m, wkv_ref[:, :D_MODEL], preferred_element_type=F32).astype(BF16)
    v_ref[...] = jnp.dot(m, wkv_ref[:, D_MODEL:], preferred_element_type=F32).astype(BF16)


def _mem_kv(mem, gains, wkv, layer):
    b, m, _ = mem.shape
    blk = pl.BlockSpec((None, m, D_MODEL), lambda i: (i, 0, 0))
    return pl.pallas_call(
        _mem_kv_kernel,
        grid=(b,),
        in_specs=[blk, _layer_spec(gains, layer, 1), _layer_spec(wkv, layer, 1)],
        out_specs=[blk, blk],
        out_shape=[jax.ShapeDtypeStruct((b, m, D_MODEL), BF16)] * 2,
        compiler_params=pltpu.CompilerParams(dimension_semantics=("parallel",),
                                             vmem_limit_bytes=VMEM_LIMIT),
        name="mem_kv",
    )(mem, gains, wkv)


def _token_kernel(x_ref, ya_ref, yb_ref, yc_ref, yd_ref, k_ref, v_ref, wmix_ref, gx_ref, wq_ref, wo_ref,
                  gm_ref, w1_ref, w2_ref, fg_ref, o_ref, *, final_norm):
    def part(rs):
        x = x_ref[rs, :]
        for gi, y_ref in enumerate((ya_ref, yb_ref, yc_ref, yd_ref)):
            x = x + jnp.dot(y_ref[rs, :].astype(BF16), wmix_ref[gi * GROUP_WIDTH:(gi + 1) * GROUP_WIDTH, :],
                            preferred_element_type=F32)
        yield
        hq = (_rms(x, D_MODEL) * gx_ref[...]).astype(BF16)
        q = jnp.dot(hq, wq_ref[...], preferred_element_type=F32).astype(BF16)
        yield
        heads = [slice(h * XA_HEAD_DIM, (h + 1) * XA_HEAD_DIM) for h in range(XA_HEADS)]
        scores = [lax.dot_general(q[:, sl], k_ref[:, sl], (((1,), (1,)), ((), ())),
                                  preferred_element_type=F32) for sl in heads]
        yield
        outs = []
        for h, sl in enumerate(heads):
            sc = scores[h] * (XA_HEAD_DIM ** -0.5)
            e = jnp.exp(sc - jnp.max(sc, axis=-1, keepdims=True))
            p = e / jnp.sum(e, axis=-1, keepdims=True)
            outs.append(jnp.dot(p.astype(BF16), v_ref[:, sl], preferred_element_type=F32).astype(BF16))
        yield
        x = x + jnp.dot(jnp.concatenate(outs, axis=1), wo_ref[...], preferred_element_type=F32)
        yield
        hm = (_rms(x, D_MODEL) * gm_ref[...]).astype(BF16)
        slabs = [slice(c * FF_SLAB, (c + 1) * FF_SLAB) for c in range(D_FF // FF_SLAB)]
        a = jnp.dot(hm, w1_ref[:, slabs[0]], preferred_element_type=F32)
        yield
        for c in range(1, len(slabs)):
            a_next = jnp.dot(hm, w1_ref[:, slabs[c]], preferred_element_type=F32)
            a = jnp.maximum(a, 0.0)
            x = x + jnp.dot((a * a).astype(BF16), w2_ref[slabs[c - 1], :], preferred_element_type=F32)
            a = a_next
            yield
        a = jnp.maximum(a, 0.0)
        x = x + jnp.dot((a * a).astype(BF16), w2_ref[slabs[-1], :], preferred_element_type=F32)
        if final_norm:
            x = _rms(x, D_MODEL) * fg_ref[...]
        o_ref[rs, :] = x

    _interleave(*[part(slice(r, r + TOKEN_SUB)) for r in range(0, TOKEN_BLOCK, TOKEN_SUB)])


def _token_mix(x, ys, k, v, weights, final_gain, layer, final_norm):
    b, s, _ = x.shape
    m = k.shape[1]
    row = lambda w: pl.BlockSpec((None, TOKEN_BLOCK, w), lambda i, j: (i, j, 0))
    mem = pl.BlockSpec((None, m, D_MODEL), lambda i, j: (i, 0, 0))
    const = lambda a: _layer_spec(a, layer, 2, pipeline_mode=pl.Buffered(1))
    final = pl.BlockSpec(final_gain.shape, lambda i, j: (0, 0), pipeline_mode=pl.Buffered(1))
    return pl.pallas_call(
        functools.partial(_token_kernel, final_norm=final_norm),
        grid=(b, s // TOKEN_BLOCK),
        in_specs=[row(D_MODEL)] + [row(GROUP_WIDTH)] * 4 + [mem, mem] + [const(a) for a in weights] + [final],
        out_specs=row(D_MODEL),
        out_shape=jax.ShapeDtypeStruct((b, s, D_MODEL), F32),
        compiler_params=pltpu.CompilerParams(dimension_semantics=("parallel", "parallel"),
                                             vmem_limit_bytes=VMEM_LIMIT),
        name="token_mix",
    )(x, *ys, k, v, *weights, final_gain)


def _pack_w_in(w):
    w = w.astype(BF16)
    zeros = lambda n: jnp.zeros(w.shape[:-1] + (n,), w.dtype)
    c0 = ZA_W + ZB_W
    cq = w[..., c0:c0 + MLA_Q_LORA]
    ckv = w[..., c0 + MLA_Q_LORA:c0 + MLA_Q_LORA + MLA_KV_LORA]
    kr = w[..., c0 + MLA_Q_LORA + MLA_KV_LORA:c0 + MLA_Q_LORA + MLA_KV_LORA + MLA_ROPE]
    d0 = c0 + MLA_Q_LORA + MLA_KV_LORA + MLA_ROPE
    rest = w[..., d0:]
    packed = jnp.concatenate([
        w[..., :c0],
        cq, zeros(ZC_KV - MLA_Q_LORA), ckv, zeros(MLA_NOPE), kr, zeros(LANES - MLA_NOPE - MLA_ROPE),
        rest, zeros(ZD_W - rest.shape[-1]),
    ], axis=-1)
    assert packed.shape[-1] == Z_W
    return packed.astype(BF16)


def kernel(x, mem, positions, norm_mix, w_in, ret_gn, lru_conv_w, lru_conv_b, lru_wa, lru_ba, lru_wx, lru_bx, lru_lambda, mla_q_norm, mla_w_uq, mla_kv_norm, mla_w_ukv, gdn_conv_w, gdn_a_log, gdn_dt_bias, gdn_norm, w_out, norm_xattn, norm_mem, xa_wq, xa_wkv, xa_wo, norm_mlp, mlp_w1, mlp_w2, final_norm):
    b, s, d = x.shape
    depth = w_in.shape[0]
    pos_f = positions.astype(F32)
    rows = lambda g: g[:, None, :]
    w_in_packed = _pack_w_in(w_in)
    wkv = xa_wkv.astype(BF16)
    token_weights = (w_out.astype(BF16), rows(norm_xattn), xa_wq.astype(BF16), xa_wo.astype(BF16),
                     rows(norm_mlp), mlp_w1.astype(BF16), mlp_w2.astype(BF16))
    for l in range(depth):
        za, zb, zc, zd = _in_proj(x.reshape(b * s, d), rows(norm_mix), w_in_packed, l)
        seq = lambda z: z.reshape(b, s, z.shape[-1])
        y_a, y_c = _ret_mla(seq(za), seq(zc), pos_f, ret_gn[l], mla_q_norm[l], mla_w_uq[l], mla_kv_norm[l],
                            mla_w_ukv[l])
        y_d, y_b = _gdn_lru(seq(zd), seq(zb), gdn_conv_w[l], gdn_a_log[l], gdn_dt_bias[l], gdn_norm[l],
                            lru_conv_w[l], lru_conv_b[l], lru_wa[l], lru_ba[l], lru_wx[l], lru_bx[l],
                            lru_lambda[l])
        k, v = _mem_kv(mem, rows(norm_mem), wkv, l)
        x = _token_mix(x, (y_a, y_b, y_c, y_d), k, v, token_weights, final_norm[None, :], l, l == depth - 1)
    return x
```

```python
import functools

import jax
import jax.numpy as jnp
from jax import lax
from jax.experimental import pallas as pl
from jax.experimental.pallas import tpu as pltpu

F32 = jnp.float32
BF16 = jnp.bfloat16

D_MODEL = 1024
CHUNK = 64
HEADS = 4
HEAD_DIM = 64
GROUP_WIDTH = HEADS * HEAD_DIM
EPS = 1e-6
LOG2_E = 1.4426950408889634
ROPE_THETA = 10000.0
LRU_C = 8.0
MLA_Q_LORA = 192
MLA_KV_LORA = 128
MLA_ROPE = 32
MLA_NOPE = 64
XA_HEADS = 4
XA_HEAD_DIM = D_MODEL // XA_HEADS
D_FF = 4 * D_MODEL
FF_SLAB = 1024
ATT_TILE = 512
ROPE_TILE = 256
GDN_ROWS = 256
SOLVE_BASE = 16
assert GDN_ROWS == GROUP_WIDTH and HEAD_DIM == CHUNK
ROW_TILE = 512
TOKEN_BLOCK = 512
TOKEN_SUB = 256
SUBLANES = 8
LANES = 128
VMEM_LIMIT = 56 * 1024 * 1024

ZA_W = 4 * GROUP_WIDTH
ZB_W = 2 * GROUP_WIDTH
ZC_W = 512
ZD_W = 1152
ZC_KV = 256
ZC_KR = 384


def _dot(a, b):
    return jnp.dot(a.astype(BF16), b.astype(BF16), preferred_element_type=F32)


def _dot_nt(a, b):
    return lax.dot_general(a.astype(BF16), b.astype(BF16), (((1,), (1,)), ((), ())),
                           preferred_element_type=F32)


def _dot_tn(a, b):
    return lax.dot_general(a.astype(BF16), b.astype(BF16), (((0,), (0,)), ((), ())),
                           preferred_element_type=F32)


def _silu(x):
    return x * jax.nn.sigmoid(x)


def _softplus(x):
    return jnp.maximum(x, 0.0) + jnp.log1p(jnp.exp(-jnp.abs(x)))


def _expm1(y):
    e = jnp.exp(y)
    near = (y > -0.5) & (e != 1.0)
    safe_log = jnp.where(near, jnp.log(e), 1.0)
    return jnp.where(near, (e - 1.0) * y / safe_log, jnp.where(e == 1.0, y, e - 1.0))


def _rms(x, n):
    return x * lax.rsqrt(jnp.sum(x * x, axis=-1, keepdims=True) * (1.0 / n) + EPS)


def _layer_spec(stacked, layer, grid_rank, **kwargs):
    zeros = (0,) * (stacked.ndim - 1)
    index_map = (lambda i: (layer,) + zeros) if grid_rank == 1 else (lambda i, j: (layer,) + zeros)
    return pl.BlockSpec((None,) + stacked.shape[1:], index_map, **kwargs)


def _chunk_rows(n):
    return pl.ds(pl.multiple_of(n * CHUNK, CHUNK), CHUNK)


def _causal_conv_rows(z_ref, n, nrows, lo, hi, w):
    cur = z_ref[pl.ds(pl.multiple_of(n * nrows, nrows), nrows), lo:hi]
    prev_start = pl.multiple_of(jnp.maximum(n * nrows - SUBLANES, 0), SUBLANES)
    prev = z_ref[pl.ds(prev_start, SUBLANES), lo:hi]
    prev = jnp.where(n > 0, prev, 0.0)
    tile = jnp.concatenate([prev, cur], axis=0)
    acc = cur * w[3:4, :]
    for k in (1, 2, 3):
        acc = acc + pltpu.roll(tile, k, 0)[SUBLANES:, :] * w[3 - k:4 - k, :]
    return acc


def _lane_block_swap(x, half, first):
    return jnp.where(first, pltpu.roll(x, LANES - half, 1), pltpu.roll(x, half, 1))


def _in_proj_kernel(x_ref, g_ref, w_ref, wcd_ref, za_ref, zb_ref, zc_ref, zd_ref):
    x = x_ref[...]
    h = (_rms(x, D_MODEL) * g_ref[...]).astype(BF16)
    za_ref[...] = jnp.dot(h, w_ref[:, 0:ZA_W], preferred_element_type=F32)
    zb_ref[...] = jnp.dot(h, w_ref[:, ZA_W:ZA_W + ZB_W], preferred_element_type=F32)
    zc_ref[...] = jnp.dot(h, wcd_ref[:, 0:ZC_W], preferred_element_type=F32)
    zd_ref[...] = jnp.dot(h, wcd_ref[:, ZC_W:ZC_W + ZD_W], preferred_element_type=F32)


def _in_proj(x2d, gains, w_bf16, w_cd, layer):
    t = x2d.shape[0]
    row = lambda w: pl.BlockSpec((ROW_TILE, w), lambda i: (i, 0))
    return pl.pallas_call(
        _in_proj_kernel,
        grid=(t // ROW_TILE,),
        in_specs=[row(D_MODEL), _layer_spec(gains, layer, 1), _layer_spec(w_bf16, layer, 1),
                  _layer_spec(w_cd, layer, 1)],
        out_specs=[row(ZA_W), row(ZB_W), row(ZC_W), row(ZD_W)],
        out_shape=[jax.ShapeDtypeStruct((t, w), F32) for w in (ZA_W, ZB_W, ZC_W, ZD_W)],
        compiler_params=pltpu.CompilerParams(dimension_semantics=("parallel",),
                                             vmem_limit_bytes=VMEM_LIMIT),
        name="in_proj",
    )(x2d, gains, w_bf16, w_cd)


def _compact_positions(pos_f, n_freq):
    b, s = pos_f.shape
    nb = LANES // n_freq
    r = ROPE_TILE // nb
    p = pos_f.reshape(b, s // ROPE_TILE, nb, r).transpose(0, 1, 3, 2)
    return jnp.repeat(p, n_freq, axis=-1)


def _rope_expanders(n_freq, width, rope_lo, rope_hi):
    nb = LANES // n_freq
    src = jnp.arange(LANES)
    lane = jnp.arange(width)
    on_rope = (lane >= rope_lo) & (lane < rope_hi)
    hit = (src[:, None] % n_freq == lane[None, :] % n_freq) & on_rope[None, :]
    return jnp.stack([(hit & (src[:, None] // n_freq == r)) for r in range(nb)]).astype(BF16)


def _expand_rope_table(t_c, e_ref):
    return jnp.concatenate([_dot_exact(t_c, e_ref[r], 3) for r in range(e_ref.shape[0])], axis=0)


def _retention_stages(z_ref, posc_ref, invf_ref, e_ref, dmat_ref, xi_ref, zeta_ref, cd_ref, gn_ref,
                      o_ref, state_ref):
    gw = GROUP_WIDTH
    n_rows = ROPE_TILE
    lane = lax.broadcasted_iota(jnp.int32, (n_rows, LANES), 1)
    first = (lane % HEAD_DIM) < (HEAD_DIM // 2)
    lane_wide = lax.broadcasted_iota(jnp.int32, (n_rows, gw), 1)
    first_wide = (lane_wide % HEAD_DIM) < (HEAD_DIM // 2)
    ri = lax.broadcasted_iota(jnp.int32, (gw, gw), 0)
    ci = lax.broadcasted_iota(jnp.int32, (gw, gw), 1)
    same_head = (ri // HEAD_DIM) == (ci // HEAD_DIM)
    ones_bd = same_head.astype(BF16)
    state_ref[...] = jnp.zeros_like(state_ref)

    def rope(x, cos, sin_signed):
        parts = []
        for c in range(gw // LANES):
            sl = slice(c * LANES, (c + 1) * LANES)
            xs = _lane_block_swap(x[:, sl], HEAD_DIM // 2, first)
            parts.append(x[:, sl] * cos[:, sl] + xs * sin_signed[:, sl])
        return jnp.concatenate(parts, axis=1)

    def body(g):
        rows = pl.ds(pl.multiple_of(g * n_rows, n_rows), n_rows)
        ang = posc_ref[g] * invf_ref[...]
        cos = _expand_rope_table(jnp.cos(ang), e_ref)
        sin = _expand_rope_table(jnp.sin(ang), e_ref)
        yield
        sin_signed = jnp.where(first_wide, -sin, sin)
        q = (rope(z_ref[rows, 0:gw], cos, sin_signed) * (HEAD_DIM ** -0.5)).astype(BF16)
        k = rope(z_ref[rows, gw:2 * gw], cos, sin_signed)
        v = z_ref[rows, 2 * gw:3 * gw].astype(BF16)
        kz = (k * zeta_ref[...]).astype(BF16)
        k = k.astype(BF16)
        heads = [slice(h * HEAD_DIM, (h + 1) * HEAD_DIM) for h in range(HEADS)]
        blocks = [slice(c * CHUNK, (c + 1) * CHUNK) for c in range(n_rows // CHUNK)]
        scores = [_dot_nt(q[:, sl], k[:, sl]) for sl in heads]
        yield
        kvs = [_dot_tn(kz[blk], v[blk]) for blk in blocks]
        yield
        intra = [_dot(scores[h] * dmat_ref[h], v[:, sl])
                 for h, sl in enumerate(heads)]
        yield
        st = state_ref[...]
        cross = []
        for c, blk in enumerate(blocks):
            cross.append(_dot(q[blk], st))
            st = st * cd_ref[...] + jnp.where(same_head, kvs[c], 0.0)
        state_ref[...] = st
        yield
        o = jnp.concatenate(intra, axis=1) + jnp.concatenate(cross, axis=0) * xi_ref[...]
        mu = _dot_exact(o, ones_bd, 2) * (1.0 / HEAD_DIM)
        yield
        d = o - mu
        var = _dot_exact(d * d, ones_bd, 2) * (1.0 / HEAD_DIM)
        gate = z_ref[rows, 3 * gw:4 * gw]
        o_ref[rows, :] = d * lax.rsqrt(var + EPS) * gn_ref[...] * _silu(gate)

    return body


def _retention_consts(pos_f, ret_gn):
    h = HEADS
    half = HEAD_DIM // 2
    per = ROPE_TILE // CHUNK
    inv_freq = jnp.power(ROPE_THETA, -jnp.arange(half, dtype=F32) / half)
    invf = jnp.tile(inv_freq, LANES // half)[None, :]
    posc = _compact_positions(pos_f, half)
    expand = _rope_expanders(half, GROUP_WIDTH, 0, GROUP_WIDTH)
    log_gamma = jnp.log1p(-jnp.exp2(-5.0 - jnp.arange(h, dtype=F32)))
    idx = jnp.arange(CHUNK, dtype=F32)
    dmat = jnp.exp(jnp.abs(idx[:, None] - idx[None, :])[None] * log_gamma[:, None, None])
    dmat = jnp.stack([jnp.kron(jnp.eye(per, dtype=F32), dmat[i]) for i in range(h)])
    xi = jnp.repeat(jnp.exp((idx[:, None] + 1.0) * log_gamma), HEAD_DIM, axis=1)
    zeta = jnp.repeat(jnp.exp((CHUNK - 1.0 - idx)[:, None] * log_gamma), HEAD_DIM, axis=1)
    xi = jnp.tile(xi, (per, 1))
    zeta = jnp.tile(zeta, (per, 1))
    cd = jnp.repeat(jnp.exp(CHUNK * log_gamma), HEAD_DIM)[None, :]
    gn = ret_gn.reshape(1, GROUP_WIDTH)
    return posc, (invf, expand, dmat, xi, zeta, cd, gn)


def _interleave(*stage_generators):
    live = list(stage_generators)
    while live:
        for gen in list(live):
            try:
                next(gen)
            except StopIteration:
                live.remove(gen)


def _lru_stages(gi, z_ref, cw_ref, cb_ref, wa_ref, ba_ref, wx_ref, bx_ref, lam_ref, o_ref, h_ref):
    w = GROUP_WIDTH
    row = lax.broadcasted_iota(jnp.int32, (CHUNK, w), 0)
    for c in range(GDN_ROWS // CHUNK):
        n = gi * (GDN_ROWS // CHUNK) + c
        rows = _chunk_rows(n)
        xc = _causal_conv_rows(z_ref, n, CHUNK, 0, w, cw_ref[...]) + cb_ref[...]
        r = jax.nn.sigmoid(_dot(xc, wa_ref[...]) + ba_ref[...])
        i = jax.nn.sigmoid(_dot(xc, wx_ref[...]) + bx_ref[...])
        log_a = -LRU_C * r * _softplus(-lam_ref[...])
        a = jnp.exp(log_a)
        u = jnp.sqrt(-_expm1(2.0 * log_a)) * (i * xc)
        d = 1
        while d < CHUNK:
            a_sh = jnp.where(row >= d, pltpu.roll(a, d, 0), 1.0)
            u_sh = jnp.where(row >= d, pltpu.roll(u, d, 0), 0.0)
            u = a * u_sh + u
            a = a * a_sh
            d *= 2
        hs = u + a * h_ref[...]
        h_ref[...] = hs[CHUNK - 1:CHUNK, :]
        gate = z_ref[rows, w:2 * w]
        o_ref[rows, :] = hs * jax.nn.gelu(gate, approximate=True)
        yield


def _mla_parts(z_ref, posc_ref, invf_ref, e_ref, base_ref, qn_ref, wq_ref, kvn_ref, wk_ref, wvt_ref,
               o_ref, q_s, k_s, vt_s):
    s = z_ref.shape[0]
    nt = s // ATT_TILE
    exp2_coef = (MLA_NOPE + MLA_ROPE) ** -0.5 * LOG2_E
    lane = lax.broadcasted_iota(jnp.int32, (ROPE_TILE, LANES), 1)
    first = (lane >= MLA_NOPE) & (lane < MLA_NOPE + MLA_ROPE // 2)

    def rope(x, cos, sin_signed):
        return x * cos + _lane_block_swap(x, MLA_ROPE // 2, first) * sin_signed

    def project(t):
        rows = pl.ds(pl.multiple_of(t * ROPE_TILE, ROPE_TILE), ROPE_TILE)
        ang = posc_ref[t] * invf_ref[...]
        cos = _expand_rope_table(jnp.cos(ang), e_ref) + base_ref[...]
        sin = _expand_rope_table(jnp.sin(ang), e_ref)
        yield
        sin_signed = jnp.where(first, -sin, sin)
        cq = (_rms(z_ref[rows, 0:ZC_KV], MLA_Q_LORA) * qn_ref[...]).astype(BF16)
        ckv_f = _rms(z_ref[rows, ZC_KV:ZC_KR], MLA_KV_LORA) * kvn_ref[...]
        ckv = ckv_f.astype(BF16)
        k_rope = rope(z_ref[rows, ZC_KR:ZC_W], cos, sin_signed)
        vt = jnp.dot(wvt_ref[...], ckv_f.T.astype(BF16), preferred_element_type=F32)
        qs = [jnp.dot(cq, wq_ref[h], preferred_element_type=F32) for h in range(HEADS)]
        yield
        ks = [jnp.dot(ckv, wk_ref[h], preferred_element_type=F32) for h in range(HEADS)]
        vt_s[:, rows] = vt.astype(BF16)
        yield
        for h in range(HEADS):
            q_s[h, rows, :] = rope(qs[h], cos, sin_signed).astype(BF16)
            k_s[h, rows, :] = (ks[h] + k_rope).astype(BF16)

    key_id = lax.broadcasted_iota(jnp.int32, (ATT_TILE, ATT_TILE), 0) // CHUNK
    qry_id = lax.broadcasted_iota(jnp.int32, (ATT_TILE, ATT_TILE), 1) // CHUNK
    diag_visible = key_id <= qry_id

    def q_tile(i, carry):
        rows = pl.ds(pl.multiple_of(i * ATT_TILE, ATT_TILE), ATT_TILE)

        def kv_tile(j, state, on_diagonal):
            cols = pl.ds(pl.multiple_of(j * ATT_TILE, ATT_TILE), ATT_TILE)
            scores = [lax.dot_general(k_s[h, cols, :], q_s[h, rows, :], (((1,), (1,)), ((), ())),
                                      preferred_element_type=F32) for h in range(HEADS)]
            new_state = []
            for h in range(HEADS):
                m, l, acc = state[h]
                sc = scores[h]
                if on_diagonal:
                    sc = jnp.where(diag_visible, sc, -jnp.inf)
                m_new = jnp.maximum(m, jnp.max(sc, axis=0, keepdims=True))
                p = jnp.exp2((sc - m_new) * exp2_coef)
                alpha = jnp.exp2((m - m_new) * exp2_coef)
                l = alpha * l + jnp.sum(p, axis=0, keepdims=True)
                vt = vt_s[h * HEAD_DIM:(h + 1) * HEAD_DIM, cols]
                acc = alpha * acc + jnp.dot(vt, p.astype(BF16), preferred_element_type=F32)
                new_state.append((m_new, l, acc))
            return tuple(new_state)

        init = tuple((jnp.full((1, ATT_TILE), -jnp.inf, F32), jnp.zeros((1, ATT_TILE), F32),
                      jnp.zeros((HEAD_DIM, ATT_TILE), F32)) for _ in range(HEADS))
        state = lax.fori_loop(0, i, lambda j, st: kv_tile(j, st, False), init)
        state = kv_tile(i, state, True)
        o_t = jnp.concatenate([acc / l for _, l, acc in state], axis=0)
        o_ref[rows, :] = o_t.T
        return carry

    def attention():
        lax.fori_loop(0, nt, q_tile, 0)

    return project, attention


def _mla_consts(pos_f, q_norm, w_uq, kv_norm, w_ukv):
    h = HEADS
    half = MLA_ROPE // 2
    inv_freq = jnp.power(ROPE_THETA, -jnp.arange(half, dtype=F32) / half)
    invf = jnp.tile(inv_freq, LANES // half)[None, :]
    posc = _compact_positions(pos_f, half)
    expand = _rope_expanders(half, LANES, MLA_NOPE, MLA_NOPE + MLA_ROPE)
    lane = jnp.arange(LANES)
    base = ((lane < MLA_NOPE) | (lane >= MLA_NOPE + MLA_ROPE)).astype(F32)[None, :]
    qn = jnp.pad(q_norm, (0, ZC_KV - MLA_Q_LORA))[None, :]
    kvn = kv_norm.reshape(1, MLA_KV_LORA)
    wq = w_uq.reshape(MLA_Q_LORA, h, MLA_NOPE + MLA_ROPE).transpose(1, 0, 2)
    wq = jnp.pad(wq, ((0, 0), (0, ZC_KV - MLA_Q_LORA), (0, LANES - MLA_NOPE - MLA_ROPE))).astype(BF16)
    wkv = w_ukv.reshape(MLA_KV_LORA, h, MLA_NOPE + HEAD_DIM)
    wk = jnp.pad(wkv[:, :, :MLA_NOPE].transpose(1, 0, 2),
                 ((0, 0), (0, 0), (0, LANES - MLA_NOPE))).astype(BF16)
    wvt = wkv[:, :, MLA_NOPE:].reshape(MLA_KV_LORA, h * HEAD_DIM).T.astype(BF16)
    return posc, (invf, expand, base, qn, wq, kvn, wk, wvt)


RET_UNROLL = 4
N_RET_CONSTS = 7
N_MLA_CONSTS = 8


def _ret_mla_kernel(*refs):
    za_ref, pa_ref = refs[0:2]
    ret_consts = refs[2:2 + N_RET_CONSTS]
    zc_ref, pc_ref = refs[2 + N_RET_CONSTS:4 + N_RET_CONSTS]
    mla_consts = refs[4 + N_RET_CONSTS:4 + N_RET_CONSTS + N_MLA_CONSTS]
    oa_ref, oc_ref, state_ref, q_s, k_s, vt_s = refs[4 + N_RET_CONSTS + N_MLA_CONSTS:]
    retention = _retention_stages(za_ref, pa_ref, *ret_consts, oa_ref, state_ref)
    project, attention = _mla_parts(zc_ref, pc_ref, *mla_consts, oc_ref, q_s, k_s, vt_s)

    def groups(t, carry):
        gs = [RET_UNROLL * t + u for u in range(RET_UNROLL)]
        _interleave(*[retention(g) for g in gs], *[project(g) for g in gs])
        return carry

    n_groups = za_ref.shape[0] // ROPE_TILE
    assert n_groups % RET_UNROLL == 0
    lax.fori_loop(0, n_groups // RET_UNROLL, groups, 0)
    attention()


def _ret_mla(za, zc, pos_f, ret_gn, q_norm, w_uq, kv_norm, w_ukv):
    b, s, _ = za.shape
    pos_a, ret_consts = _retention_consts(pos_f, ret_gn)
    pos_c, mla_consts = _mla_consts(pos_f, q_norm, w_uq, kv_norm, w_ukv)
    assert len(ret_consts) == N_RET_CONSTS and len(mla_consts) == N_MLA_CONSTS
    full = lambda a: pl.BlockSpec(a.shape, lambda i: (0,) * a.ndim)
    seq = lambda w: pl.BlockSpec((None, s, w), lambda i: (i, 0, 0))
    tiles = lambda p: pl.BlockSpec((None,) + p.shape[1:], lambda i: (i, 0, 0, 0))
    return pl.pallas_call(
        _ret_mla_kernel,
        grid=(b,),
        in_specs=[seq(ZA_W), tiles(pos_a)] + [full(a) for a in ret_consts]
                 + [seq(ZC_W), tiles(pos_c)] + [full(a) for a in mla_consts],
        out_specs=[seq(GROUP_WIDTH), seq(GROUP_WIDTH)],
        out_shape=[jax.ShapeDtypeStruct((b, s, GROUP_WIDTH), F32)] * 2,
        scratch_shapes=[pltpu.VMEM((GROUP_WIDTH, GROUP_WIDTH), F32),
                        pltpu.VMEM((HEADS, s, LANES), BF16), pltpu.VMEM((HEADS, s, LANES), BF16),
                        pltpu.VMEM((GROUP_WIDTH, s), BF16)],
        compiler_params=pltpu.CompilerParams(dimension_semantics=("parallel",),
                                             vmem_limit_bytes=VMEM_LIMIT),
        name="ret_mla",
    )(za, pos_a, *ret_consts, zc, pos_c, *mla_consts)


def _bf16_parts(x, parts):
    out = []
    for _ in range(parts):
        p = x.astype(BF16)
        out.append(p)
        x = x - p.astype(F32)
    return out


def _dot_exact(a, b, parts=None):
    f32_is_lhs = isinstance(a, list) or a.dtype == F32
    split = a if f32_is_lhs else b
    if not isinstance(split, list):
        split = _bf16_parts(split, parts)
    terms = [jnp.dot(p, b, preferred_element_type=F32) if f32_is_lhs
             else jnp.dot(a, p, preferred_element_type=F32) for p in split]
    acc = terms[-1]
    for t in terms[-2::-1]:
        acc = acc + t
    return acc


def _gdn_lru_kernel(z_ref, cw_ref, alog_ref, dtb_ref, ng_ref,
                    zb_ref, lcw_ref, lcb_ref, lwa_ref, lba_ref, lwx_ref, lbx_ref, llam_ref,
                    o_ref, ob_ref, u_s, wq_s, kd_s, attn_s, gl_s, state_ref, h_ref):
    s = z_ref.shape[0]
    gw = GROUP_WIDTH
    g_rows = GDN_ROWS
    per = g_rows // CHUNK
    ri = lax.broadcasted_iota(jnp.int32, (g_rows, g_rows), 0)
    ci = lax.broadcasted_iota(jnp.int32, (g_rows, g_rows), 1)
    block_mask = lambda width: (ri // width) == (ci // width)
    eye = ri == ci
    same_block = block_mask(CHUNK)
    causal = same_block & (ri >= ci)
    strict = same_block & (ri > ci)
    tri = causal.astype(BF16)
    ones_bd = same_block.astype(BF16)
    er = lax.broadcasted_iota(jnp.int32, (LANES, gw), 0)
    ec = lax.broadcasted_iota(jnp.int32, (LANES, gw), 1) // HEAD_DIM
    expand_beta = (er == ec).astype(BF16)
    expand_alpha = (er == ec + HEADS).astype(BF16)

    def precompute(gi):
        rows = pl.ds(pl.multiple_of(gi * g_rows, g_rows), g_rows)
        qkv = _silu(_causal_conv_rows(z_ref, gi, g_rows, 0, 3 * gw, cw_ref[...]))
        q = qkv[:, 0:gw]
        k = qkv[:, gw:2 * gw]
        v = qkv[:, 2 * gw:3 * gw]
        small = z_ref[rows, 4 * gw:4 * gw + LANES]
        beta_t = jax.nn.sigmoid(small)
        g_t = -jnp.exp(alog_ref[...]) * _softplus(small + dtb_ref[...])
        qn = q * lax.rsqrt(_dot_exact(q * q, ones_bd, 2) + EPS) * (HEAD_DIM ** -0.5)
        kn = k * lax.rsqrt(_dot_exact(k * k, ones_bd, 2) + EPS)
        beta_w = _dot_exact(beta_t, expand_beta, 3)
        g_t_parts = _bf16_parts(g_t, 3)
        g_w = _dot_exact(g_t_parts, expand_alpha)
        gc_t = _dot_exact(tri, g_t_parts)
        yield
        g_w_parts = _bf16_parts(g_w, 3)
        gc_w = _dot_exact(tri, g_w_parts)
        gl_w = _dot_exact(ones_bd, g_w_parts)
        gc_tt = gc_t.T
        kb_w = kn * beta_w
        vb_w = v * beta_w
        heads = [slice(h * HEAD_DIM, (h + 1) * HEAD_DIM) for h in range(HEADS)]
        kh = [kn[:, sl].astype(BF16) for sl in heads]
        grams = [_dot_nt(kb_w[:, sl], kh[h]) for h, sl in enumerate(heads)]
        yield
        qks = [_dot_nt(qn[:, sl], kh[h]) for h, sl in enumerate(heads)]
        egc_w = jnp.exp(gc_w)
        kbe_w = kb_w * egc_w
        yield
        lowers, rhs = [], []
        for h, sl in enumerate(heads):
            gcol = jnp.broadcast_to(gc_t[:, HEADS + h:HEADS + h + 1], (g_rows, g_rows))
            grow = jnp.broadcast_to(gc_tt[HEADS + h:HEADS + h + 1, :], (g_rows, g_rows))
            decay = jnp.exp(jnp.where(causal, gcol - grow, -jnp.inf))
            lowers.append(jnp.where(strict, grams[h] * decay, 0.0))
            attn_s[h, rows, :] = (qks[h] * decay).astype(BF16)
            rhs.append(jnp.concatenate([vb_w[:, sl], kbe_w[:, sl]], axis=1))
        bf = lambda xs: [x.astype(BF16) for x in xs]
        plus_eye = lambda x: jnp.where(eye, x + 1.0, x)
        inner = block_mask(SOLVE_BASE)
        d = [jnp.where(inner, low, 0.0) for low in lowers]
        factors = [[jnp.where(eye, 1.0, -x) for x in d]]
        power = bf(d)
        width = 2
        while width < SOLVE_BASE:
            power_f = [_dot(p, p) for p in power]
            yield
            factors.append([plus_eye(x) for x in power_f])
            power = bf(power_f)
            width *= 2
        while len(factors) > 1:
            factors = [[_dot(x, y) for x, y in zip(factors[i], factors[i + 1])] if i + 1 < len(factors)
                       else factors[i] for i in range(0, len(factors), 2)]
            yield
        t = factors[0]
        width = SOLVE_BASE
        while width < CHUNK:
            outer = block_mask(2 * width)
            e = [jnp.where(outer & ~inner, low, 0.0).astype(BF16) for low in lowers]
            t_b = bf(t)
            et = [_dot(x, y) for x, y in zip(e, t_b)]
            yield
            t = [x - _dot(y, z) for x, y, z in zip(t, t_b, et)]
            yield
            inner = outer
            width *= 2
        sols = [_dot(x, y) for x, y in zip(t, rhs)]
        yield
        u_s[rows, :] = jnp.concatenate([sol[:, :HEAD_DIM] for sol in sols], axis=1)
        w_all = jnp.concatenate([sol[:, HEAD_DIM:] for sol in sols], axis=1).astype(BF16)
        qd_w = (qn * egc_w).astype(BF16)
        kd_s[rows, :] = (kn * jnp.exp(gl_w - gc_w)).astype(BF16)
        for c in range(per):
            blk = slice(c * CHUNK, (c + 1) * CHUNK)
            wq_s[gi * per + c, 0:CHUNK, :] = w_all[blk]
            wq_s[gi * per + c, CHUNK:2 * CHUNK, :] = qd_w[blk]
            gl_s[gi * per + c] = gl_w[c * CHUNK:c * CHUNK + 1, :]

    def recur(gi):
        st = state_ref[...]
        outs = []
        for c in range(per):
            n = gi * per + c
            rows = _chunk_rows(n)
            r1 = _dot(wq_s[n], st)
            yield
            v_new = (u_s[rows, :] - r1[0:CHUNK]).astype(BF16)
            ds = lax.dot_general(kd_s[rows, :], v_new, (((0,), (0,)), ((), ())),
                                 preferred_element_type=F32)
            parts = []
            for h in range(HEADS):
                sl = slice(h * HEAD_DIM, (h + 1) * HEAD_DIM)
                a = attn_s[h, rows, c * CHUNK:(c + 1) * CHUNK]
                parts.append(jnp.dot(a, v_new[:, sl], preferred_element_type=F32))
            yield
            outs.append(r1[CHUNK:2 * CHUNK] + jnp.concatenate(parts, axis=1))
            st = st * jnp.exp(gl_s[n]) + jnp.where(same_block, ds, 0.0)
        state_ref[...] = st
        rows = pl.ds(pl.multiple_of(gi * g_rows, g_rows), g_rows)
        o = jnp.concatenate(outs, axis=0)
        ms = _dot_exact(o * o, ones_bd, 2) * (1.0 / HEAD_DIM)
        gate = z_ref[rows, 3 * gw:4 * gw]
        o_ref[rows, :] = o * lax.rsqrt(ms + EPS) * ng_ref[...] * _silu(gate)

    def lru(gi):
        return _lru_stages(gi, zb_ref, lcw_ref, lcb_ref, lwa_ref, lba_ref, lwx_ref, lbx_ref, llam_ref,
                           ob_ref, h_ref)

    state_ref[...] = jnp.zeros_like(state_ref)
    h_ref[...] = jnp.zeros_like(h_ref)
    n_groups = s // g_rows
    _interleave(precompute(0))

    def steady(gi, carry):
        _interleave(recur(gi - 1), precompute(gi), lru(gi - 1))
        return carry

    lax.fori_loop(1, n_groups, steady, 0)
    _interleave(recur(n_groups - 1), lru(n_groups - 1))


def _gdn_lru(zd, zb, conv_w, a_log, dt_bias, norm_g, lru_conv_w, lru_conv_b, wa, ba, wx, bx, lam):
    b, s, _ = zd.shape
    lane_vec = lambda p: jnp.pad(p, (HEADS, LANES - 2 * HEADS))[None, :]
    ng = jnp.tile(norm_g, HEADS)[None, :]
    eye = jnp.eye(wa.shape[0], dtype=wa.dtype)
    bd = lambda m: jnp.einsum("nij,nm->nimj", m, eye).reshape(GROUP_WIDTH, GROUP_WIDTH).astype(BF16)
    vec = lambda v: v.reshape(1, GROUP_WIDTH)
    gdn_args = (conv_w, lane_vec(a_log), lane_vec(dt_bias), ng)
    lru_args = (lru_conv_w, vec(lru_conv_b), bd(wa), vec(ba), bd(wx), vec(bx), vec(lam))
    full = lambda a: pl.BlockSpec(a.shape, lambda i: (0,) * a.ndim)
    seq = lambda w: pl.BlockSpec((None, s, w), lambda i: (i, 0, 0))
    return pl.pallas_call(
        _gdn_lru_kernel,
        grid=(b,),
        in_specs=[seq(ZD_W)] + [full(a) for a in gdn_args] + [seq(ZB_W)] + [full(a) for a in lru_args],
        out_specs=[seq(GROUP_WIDTH), seq(GROUP_WIDTH)],
        out_shape=[jax.ShapeDtypeStruct((b, s, GROUP_WIDTH), F32)] * 2,
        scratch_shapes=[pltpu.VMEM((s, GROUP_WIDTH), F32),
                        pltpu.VMEM((s // CHUNK, 2 * CHUNK, GROUP_WIDTH), BF16),
                        pltpu.VMEM((s, GROUP_WIDTH), BF16),
                        pltpu.VMEM((HEADS, s, GDN_ROWS), BF16),
                        pltpu.VMEM((s // CHUNK, 1, GROUP_WIDTH), F32),
                        pltpu.VMEM((GROUP_WIDTH, GROUP_WIDTH), F32),
                        pltpu.VMEM((1, GROUP_WIDTH), F32)],
        compiler_params=pltpu.CompilerParams(dimension_semantics=("parallel",),
                                             vmem_limit_bytes=VMEM_LIMIT),
        name="gdn_lru",
    )(zd, *gdn_args, zb, *lru_args)


def _mem_kv_kernel(mem_ref, g_ref, wkv_ref, k_ref, v_ref):
    m = (_rms(mem_ref[...], D_MODEL) * g_ref[...]).astype(BF16)
    k_ref[...] = jnp.dot(m, wkv_ref[:, :D_MODEL], preferred_element_type=F32).astype(BF16)
    v_ref[...] = jnp.dot(m, wkv_ref[:, D_MODEL:], preferred_element_type=F32).astype(BF16)


def _mem_kv(mem, gains, wkv, layer):
    b, m, _ = mem.shape
    blk = pl.BlockSpec((None, m, D_MODEL), lambda i: (i, 0, 0))
    return pl.pallas_call(
        _mem_kv_kernel,
        grid=(b,),
        in_specs=[blk, _layer_spec(gains, layer, 1), _layer_spec(wkv, layer, 1)],
        out_specs=[blk, blk],
        out_shape=[jax.ShapeDtypeStruct((b, m, D_MODEL), BF16)] * 2,
        compiler_params=pltpu.CompilerParams(dimension_semantics=("parallel",),
                                             vmem_limit_bytes=VMEM_LIMIT),
        name="mem_kv",
    )(mem, gains, wkv)


def _token_kernel(x_ref, ya_ref, yb_ref, yc_ref, yd_ref, k_ref, v_ref, wmix_ref, gx_ref, wq_ref, wo_ref,
                  gm_ref, w1_ref, w2_ref, fg_ref, o_ref, *, final_norm):
    def part(rs):
        x = x_ref[rs, :]
        for gi, y_ref in enumerate((ya_ref, yb_ref, yc_ref, yd_ref)):
            x = x + jnp.dot(y_ref[rs, :].astype(BF16), wmix_ref[gi * GROUP_WIDTH:(gi + 1) * GROUP_WIDTH, :],
                            preferred_element_type=F32)
        yield
        hq = (_rms(x, D_MODEL) * gx_ref[...]).astype(BF16)
        q = jnp.dot(hq, wq_ref[...], preferred_element_type=F32).astype(BF16)
        yield
        heads = [slice(h * XA_HEAD_DIM, (h + 1) * XA_HEAD_DIM) for h in range(XA_HEADS)]
        scores = [lax.dot_general(q[:, sl], k_ref[:, sl], (((1,), (1,)), ((), ())),
                                  preferred_element_type=F32) for sl in heads]
        yield
        outs = []
        for h, sl in enumerate(heads):
            sc = scores[h] * (XA_HEAD_DIM ** -0.5)
            e = jnp.exp(sc - jnp.max(sc, axis=-1, keepdims=True))
            p = e / jnp.sum(e, axis=-1, keepdims=True)
            outs.append(jnp.dot(p.astype(BF16), v_ref[:, sl], preferred_element_type=F32).astype(BF16))
        yield
        x = x + jnp.dot(jnp.concatenate(outs, axis=1), wo_ref[...], preferred_element_type=F32)
        yield
        hm = (_rms(x, D_MODEL) * gm_ref[...]).astype(BF16)
        slabs = [slice(c * FF_SLAB, (c + 1) * FF_SLAB) for c in range(D_FF // FF_SLAB)]
        a = jnp.dot(hm, w1_ref[:, slabs[0]], preferred_element_type=F32)
        yield
        for c in range(1, len(slabs)):
            a_next = jnp.dot(hm, w1_ref[:, slabs[c]], preferred_element_type=F32)
            a = jnp.maximum(a, 0.0)
            x = x + jnp.dot((a * a).astype(BF16), w2_ref[slabs[c - 1], :], preferred_element_type=F32)
            a = a_next
            yield
        a = jnp.maximum(a, 0.0)
        x = x + jnp.dot((a * a).astype(BF16), w2_ref[slabs[-1], :], preferred_element_type=F32)
        if final_norm:
            x = _rms(x, D_MODEL) * fg_ref[...]
        o_ref[rs, :] = x

    _interleave(*[part(slice(r, r + TOKEN_SUB)) for r in range(0, TOKEN_BLOCK, TOKEN_SUB)])


def _token_mix(x, ys, k, v, weights, final_gain, layer, final_norm):
    b, s, _ = x.shape
    m = k.shape[1]
    row = lambda w: pl.BlockSpec((None, TOKEN_BLOCK, w), lambda i, j: (i, j, 0))
    mem = pl.BlockSpec((None, m, D_MODEL), lambda i, j: (i, 0, 0))
    const = lambda a: _layer_spec(a, layer, 2, pipeline_mode=pl.Buffered(1))
    final = pl.BlockSpec(final_gain.shape, lambda i, j: (0, 0), pipeline_mode=pl.Buffered(1))
    return pl.pallas_call(
        functools.partial(_token_kernel, final_norm=final_norm),
        grid=(b, s // TOKEN_BLOCK),
        in_specs=[row(D_MODEL)] + [row(GROUP_WIDTH)] * 4 + [mem, mem] + [const(a) for a in weights] + [final],
        out_specs=row(D_MODEL),
        out_shape=jax.ShapeDtypeStruct((b, s, D_MODEL), F32),
        compiler_params=pltpu.CompilerParams(dimension_semantics=("parallel", "parallel"),
                                             vmem_limit_bytes=VMEM_LIMIT),
        name="token_mix",
    )(x, *ys, k, v, *weights, final_gain)


def _pack_w_in(w):
    w = w.astype(BF16)
    zeros = lambda n: jnp.zeros(w.shape[:-1] + (n,), w.dtype)
    c0 = ZA_W + ZB_W
    cq = w[..., c0:c0 + MLA_Q_LORA]
    ckv = w[..., c0 + MLA_Q_LORA:c0 + MLA_Q_LORA + MLA_KV_LORA]
    kr = w[..., c0 + MLA_Q_LORA + MLA_KV_LORA:c0 + MLA_Q_LORA + MLA_KV_LORA + MLA_ROPE]
    d0 = c0 + MLA_Q_LORA + MLA_KV_LORA + MLA_ROPE
    rest = w[..., d0:]
    packed = jnp.concatenate([
        cq, zeros(ZC_KV - MLA_Q_LORA), ckv, zeros(MLA_NOPE), kr, zeros(LANES - MLA_NOPE - MLA_ROPE),
        rest, zeros(ZD_W - rest.shape[-1]),
    ], axis=-1)
    assert packed.shape[-1] == ZC_W + ZD_W
    return w, packed


def kernel(x, mem, positions, norm_mix, w_in, ret_gn, lru_conv_w, lru_conv_b, lru_wa, lru_ba, lru_wx, lru_bx, lru_lambda, mla_q_norm, mla_w_uq, mla_kv_norm, mla_w_ukv, gdn_conv_w, gdn_a_log, gdn_dt_bias, gdn_norm, w_out, norm_xattn, norm_mem, xa_wq, xa_wkv, xa_wo, norm_mlp, mlp_w1, mlp_w2, final_norm):
    b, s, d = x.shape
    depth = w_in.shape[0]
    pos_f = positions.astype(F32)
    rows = lambda g: g[:, None, :]
    w_in_bf16, w_in_cd = _pack_w_in(w_in)
    wkv = xa_wkv.astype(BF16)
    token_weights = (w_out.astype(BF16), rows(norm_xattn), xa_wq.astype(BF16), xa_wo.astype(BF16),
                     rows(norm_mlp), mlp_w1.astype(BF16), mlp_w2.astype(BF16))
    for l in range(depth):
        za, zb, zc, zd = _in_proj(x.reshape(b * s, d), rows(norm_mix), w_in_bf16, w_in_cd, l)
        seq = lambda z: z.reshape(b, s, z.shape[-1])
        y_a, y_c = _ret_mla(seq(za), seq(zc), pos_f, ret_gn[l], mla_q_norm[l], mla_w_uq[l], mla_kv_norm[l],
                            mla_w_ukv[l])
        y_d, y_b = _gdn_lru(seq(zd), seq(zb), gdn_conv_w[l], gdn_a_log[l], gdn_dt_bias[l], gdn_norm[l],
                            lru_conv_w[l], lru_conv_b[l], lru_wa[l], lru_ba[l], lru_wx[l], lru_bx[l],
                            lru_lambda[l])
        k, v = _mem_kv(mem, rows(norm_mem), wkv, l)
        x = _token_mix(x, (y_a, y_b, y_c, y_d), k, v, token_weights, final_norm[None, :], l, l == depth - 1)
    return x
```
